```python
import jax, jax.numpy as jnp
from jax import lax
import numpy as np

D_MODEL = 1024
BATCH = 8
SEQ = 2048
DEPTH = 2
DEC_BATCH = 128
DEC_SEQ = 4
PAST_LEN = 16384
PAGE_SIZE = 128

W_A = D_MODEL // 2
CONV_A = 3
W_B = D_MODEL // 2
POOL_WINDOWS = (2, 4, 8, 16)
N_POOL_GROUPS = len(POOL_WINDOWS)
POOL_GROUP = W_B // N_POOL_GROUPS
POOL_BUF = max(POOL_WINDOWS) - 1
W_C = D_MODEL
LRU_HEADS = 8
LRU_HEAD_DIM = W_C // LRU_HEADS
CONV_C = 4
LRU_C = 8.0
W_D = D_MODEL // 2
CONV_D = 31
N_BRANCH = 4
IN_A = 3 * W_A
IN_B = W_B
IN_C = 2 * W_C
IN_D = 2 * W_D
IN_G = N_BRANCH * D_MODEL
OFF_B = IN_A
OFF_C = OFF_B + IN_B
OFF_D = OFF_C + IN_C
OFF_G = OFF_D + IN_D
IN_TOTAL = OFF_G + IN_G
PEER_HEADS = 8
N_KEYS = 128
N_EXPERTS = N_KEYS * N_KEYS
KEY_DIM = 128
PEER_TOPK = 16
PEER_BLOCK = 256
EPS = 1e-6

kernel_name = 'hybrid_conv_pool_rglru_conformer_peer_step'


def rmsnorm(x, g):
    xf = x.astype(jnp.float32)
    y = xf * lax.rsqrt(jnp.mean(xf * xf, axis=-1, keepdims=True) + EPS)
    return (y * g.astype(jnp.float32)).astype(x.dtype)


def layernorm(x, g, b):
    xf = x.astype(jnp.float32)
    mu = jnp.mean(xf, axis=-1, keepdims=True)
    xc = xf - mu
    var = jnp.mean(xc * xc, axis=-1, keepdims=True)
    return (xc * lax.rsqrt(var + EPS) * g.astype(jnp.float32) + b.astype(jnp.float32)).astype(x.dtype)


def causal_dwconv(ext, w):
    c = ext.shape[-1]
    return lax.conv_general_dilated(ext, w[:, None, :].astype(ext.dtype), window_strides=(1,), padding='VALID',
                                    dimension_numbers=('NWC', 'WIO', 'NWC'), feature_group_count=c)


def short_conv_mixer(u, buf, w_conv):
    b_gate, c_gate, xv = jnp.split(u, 3, axis=-1)
    z = c_gate * xv
    ext = jnp.concatenate([buf.astype(z.dtype), z], axis=1)
    y = b_gate * causal_dwconv(ext, w_conv)
    return y, ext[:, -(CONV_A - 1):]


def pool_mixer(u, buf, w_grp, scale, offset):
    bsz, t, _ = u.shape
    ext = jnp.concatenate([buf.astype(u.dtype), u], axis=1)
    cs = jnp.cumsum(ext.astype(jnp.float32), axis=1)
    cs = jnp.pad(cs, ((0, 0), (1, 0), (0, 0)))
    end = cs[:, POOL_BUF + 1:]
    pos = offset + jnp.arange(t)
    means = []
    for g, w in enumerate(POOL_WINDOWS):
        sl = slice(g * POOL_GROUP, (g + 1) * POOL_GROUP)
        start = cs[:, POOL_BUF + 1 - w: POOL_BUF + 1 - w + t, sl]
        cnt = jnp.minimum(pos + 1, w).astype(jnp.float32)[None, :, None]
        means.append((end[..., sl] - start) / cnt)
    mean = jnp.stack(means, axis=2)
    p = (mean - u.astype(jnp.float32).reshape(bsz, t, N_POOL_GROUPS, POOL_GROUP)).astype(u.dtype)
    y = jnp.einsum('btgc,gcd->btgd', p, w_grp).reshape(bsz, t, W_B) * scale
    return y, ext[:, -POOL_BUF:]


def rglru_mixer(u, conv_buf, h0, w_conv, b_conv, w_a, b_a, w_i, b_i, lam):
    xc, yg = jnp.split(u, 2, axis=-1)
    ext = jnp.concatenate([conv_buf.astype(xc.dtype), xc], axis=1)
    xconv = causal_dwconv(ext, w_conv) + b_conv
    bsz, t, _ = xconv.shape
    xh = xconv.reshape(bsz, t, LRU_HEADS, LRU_HEAD_DIM)
    r = jax.nn.sigmoid(jnp.einsum('bthi,hij->bthj', xh, w_a).reshape(bsz, t, W_C) + b_a)
    ig = jax.nn.sigmoid(jnp.einsum('bthi,hij->bthj', xh, w_i).reshape(bsz, t, W_C) + b_i)
    log_a = -LRU_C * r.astype(jnp.float32) * jax.nn.softplus(-lam.astype(jnp.float32))
    a = jnp.exp(log_a)
    bx = jnp.sqrt(-jnp.expm1(2.0 * log_a)) * (ig * xconv).astype(jnp.float32)

    def step(h, ab):
        a_t, b_t = ab
        h = a_t * h + b_t
        return h, h

    h_last, hs = lax.scan(step, h0.astype(jnp.float32), (jnp.swapaxes(a, 0, 1), jnp.swapaxes(bx, 0, 1)))
    hs = jnp.swapaxes(hs, 0, 1).astype(u.dtype)
    y = hs * jax.nn.gelu(yg)
    return y, ext[:, -(CONV_C - 1):], h_last.astype(h0.dtype)


def conformer_conv_mixer(u, buf, w_conv, b_conv, ln_g, ln_b):
    val, gate = jnp.split(u, 2, axis=-1)
    z = val * jax.nn.sigmoid(gate)
    ext = jnp.concatenate([buf.astype(z.dtype), z], axis=1)
    c = causal_dwconv(ext, w_conv) + b_conv
    y = jax.nn.silu(layernorm(c, ln_g, ln_b))
    return y, ext[:, -(CONV_D - 1):]


def token_mixer(xn, bufs, offset, p):
    buf_a, buf_pool, buf_c, h0, buf_d = bufs
    bsz, t, _ = xn.shape
    u = xn @ p['w_in']
    ya, sa = short_conv_mixer(u[..., :OFF_B], buf_a, p['conv_a_w'])
    yb, sb = pool_mixer(u[..., OFF_B:OFF_C], buf_pool, p['pool_w'], p['pool_scale'], offset)
    yc, sc, hc = rglru_mixer(u[..., OFF_C:OFF_D], buf_c, h0, p['conv_c_w'], p['conv_c_b'],
                             p['lru_wa'], p['lru_ba'], p['lru_wi'], p['lru_bi'], p['lru_lambda'])
    yd, sd = conformer_conv_mixer(u[..., OFF_D:OFF_G], buf_d, p['conv_d_w'], p['conv_d_b'], p['ln_d_g'], p['ln_d_b'])
    g = jax.nn.sigmoid((u[..., OFF_G:] + p['b_gate']).astype(jnp.float32)).astype(xn.dtype)
    g = g.reshape(bsz, t, N_BRANCH, D_MODEL)
    m = (g[:, :, 0] * (ya @ p['w_pa']) + g[:, :, 1] * (yb @ p['w_pb'])
         + g[:, :, 2] * (yc @ p['w_pc']) + g[:, :, 3] * (yd @ p['w_pd']))
    return m @ p['w_out'], (sa, sb, sc, hc, sd)


def peer_ffn(xn, wq, keys, u_tab, v_tab):
    bsz, t, d = xn.shape
    n = bsz * t
    blk = min(PEER_BLOCK, n)
    n_blk = -(-n // blk)
    flat = jnp.pad(xn.reshape(n, d), ((0, n_blk * blk - n), (0, 0))).reshape(n_blk, blk, d)

    def one_block(xb):
        q = (xb @ wq).reshape(blk, PEER_HEADS, 2, KEY_DIM)
        s = jnp.einsum('thcd,hcnd->thcn', q.astype(jnp.float32), keys.astype(jnp.float32))
        sv, si = lax.top_k(s, PEER_TOPK)
        cand = (sv[:, :, 0, :, None] + sv[:, :, 1, None, :]).reshape(blk, PEER_HEADS, PEER_TOPK * PEER_TOPK)
        best, pos = lax.top_k(cand, PEER_TOPK)
        i1 = jnp.take_along_axis(si[:, :, 0], pos // PEER_TOPK, axis=-1)
        i2 = jnp.take_along_axis(si[:, :, 1], pos % PEER_TOPK, axis=-1)
        expert = i1 * N_KEYS + i2
        gate = jax.nn.softmax(best, axis=-1)
        h = jnp.einsum('thkd,td->thk', u_tab[expert], xb)
        coef = (gate * jax.nn.gelu(h.astype(jnp.float32))).astype(xb.dtype)
        return jnp.einsum('thk,thkd->td', coef, v_tab[expert])

    out = lax.map(one_block, flat).reshape(n_blk * blk, d)[:n]
    return out.reshape(bsz, t, d)


def run_group(x, bufs, offset, layer_params, final_g):
    outs = ([], [], [], [], [])
    for l in range(DEPTH):
        p = {k: v[l] for k, v in layer_params.items()}
        lb = tuple(b[l] for b in bufs)
        mix, st = token_mixer(rmsnorm(x, p['norm1_g']), lb, offset, p)
        x = x + mix
        x = x + peer_ffn(rmsnorm(x, p['norm2_g']), p['peer_wq'], p['peer_keys'], p['peer_u'], p['peer_v'])
        for acc, s in zip(outs, st):
            acc.append(s)
    y = rmsnorm(x, final_g)
    return y, tuple(jnp.stack(acc) for acc in outs)


def setup_inputs(seed: int = 0) -> dict:
    key = jax.random.key(seed)
    ks = jax.random.split(key, 40)
    nrm = jax.random.normal
    f32 = jnp.float32
    a0 = jax.random.uniform(ks[20], (DEPTH, W_C), f32, 0.9, 0.999)
    s0 = a0 ** (1.0 / LRU_C)
    lam = jnp.log(s0) - jnp.log1p(-s0)
    return {
        'x_prompt': nrm(ks[0], (BATCH, SEQ, D_MODEL), f32),
        'x_sample': nrm(ks[1], (DEC_BATCH, DEC_SEQ, D_MODEL), f32),
        'state_conv_a': nrm(ks[2], (DEPTH, DEC_BATCH, CONV_A - 1, W_A), f32),
        'state_pool': nrm(ks[3], (DEPTH, DEC_BATCH, POOL_BUF, W_B), f32),
        'state_conv_c': nrm(ks[4], (DEPTH, DEC_BATCH, CONV_C - 1, W_C), f32),
        'state_lru_h': 0.5 * nrm(ks[5], (DEPTH, DEC_BATCH, W_C), f32),
        'state_conv_d': 0.5 * nrm(ks[6], (DEPTH, DEC_BATCH, CONV_D - 1, W_D), f32),
        'norm1_g': 1.0 + 0.02 * nrm(ks[7], (DEPTH, D_MODEL), f32),
        'w_in': nrm(ks[8], (DEPTH, D_MODEL, IN_TOTAL), f32) * D_MODEL ** -0.5,
        'b_gate': 0.01 * nrm(ks[9], (DEPTH, IN_G), f32),
        'conv_a_w': nrm(ks[10], (DEPTH, CONV_A, W_A), f32) * CONV_A ** -0.5,
        'pool_w': nrm(ks[11], (DEPTH, N_POOL_GROUPS, POOL_GROUP, POOL_GROUP), f32) * POOL_GROUP ** -0.5,
        'pool_scale': 1.0 + 0.02 * nrm(ks[12], (DEPTH, W_B), f32),
        'conv_c_w': nrm(ks[13], (DEPTH, CONV_C, W_C), f32) * CONV_C ** -0.5,
        'conv_c_b': 0.01 * nrm(ks[14], (DEPTH, W_C), f32),
        'lru_wa': nrm(ks[15], (DEPTH, LRU_HEADS, LRU_HEAD_DIM, LRU_HEAD_DIM), f32) * LRU_HEAD_DIM ** -0.5,
        'lru_ba': 0.01 * nrm(ks[16], (DEPTH, W_C), f32),
        'lru_wi': nrm(ks[17], (DEPTH, LRU_HEADS, LRU_HEAD_DIM, LRU_HEAD_DIM), f32) * LRU_HEAD_DIM ** -0.5,
        'lru_bi': 0.01 * nrm(ks[18], (DEPTH, W_C), f32),
        'lru_lambda': lam,
        'conv_d_w': nrm(ks[21], (DEPTH, CONV_D, W_D), f32) * CONV_D ** -0.5,
        'conv_d_b': 0.01 * nrm(ks[22], (DEPTH, W_D), f32),
        'ln_d_g': 1.0 + 0.02 * nrm(ks[23], (DEPTH, W_D), f32),
        'ln_d_b': 0.01 * nrm(ks[24], (DEPTH, W_D), f32),
        'w_pa': nrm(ks[25], (DEPTH, W_A, D_MODEL), f32) * W_A ** -0.5,
        'w_pb': nrm(ks[26], (DEPTH, W_B, D_MODEL), f32) * W_B ** -0.5,
        'w_pc': nrm(ks[27], (DEPTH, W_C, D_MODEL), f32) * W_C ** -0.5,
        'w_pd': nrm(ks[28], (DEPTH, W_D, D_MODEL), f32) * W_D ** -0.5,
        'w_out': nrm(ks[29], (DEPTH, D_MODEL, D_MODEL), f32) * D_MODEL ** -0.5,
        'norm2_g': 1.0 + 0.02 * nrm(ks[30], (DEPTH, D_MODEL), f32),
        'peer_wq': nrm(ks[31], (DEPTH, D_MODEL, PEER_HEADS * 2 * KEY_DIM), f32) * D_MODEL ** -0.5,
        'peer_keys': nrm(ks[32], (DEPTH, PEER_HEADS, 2, N_KEYS, KEY_DIM), f32) * KEY_DIM ** -0.5,
        'peer_u': nrm(ks[33], (DEPTH, N_EXPERTS, D_MODEL), f32) * D_MODEL ** -0.5,
        'peer_v': 0.35 * nrm(ks[34], (DEPTH, N_EXPERTS, D_MODEL), f32),
        'final_g': 1.0 + 0.02 * nrm(ks[35], (D_MODEL,), f32),
    }


def reference(x_prompt, x_sample, state_conv_a, state_pool, state_conv_c, state_lru_h, state_conv_d,
              norm1_g, w_in, b_gate, conv_a_w, pool_w, pool_scale, conv_c_w, conv_c_b,
              lru_wa, lru_ba, lru_wi, lru_bi, lru_lambda, conv_d_w, conv_d_b, ln_d_g, ln_d_b,
              w_pa, w_pb, w_pc, w_pd, w_out, norm2_g, peer_wq, peer_keys, peer_u, peer_v, final_g):
    layer_params = {
        'norm1_g': norm1_g, 'w_in': w_in, 'b_gate': b_gate, 'conv_a_w': conv_a_w,
        'pool_w': pool_w, 'pool_scale': pool_scale, 'conv_c_w': conv_c_w, 'conv_c_b': conv_c_b,
        'lru_wa': lru_wa, 'lru_ba': lru_ba, 'lru_wi': lru_wi, 'lru_bi': lru_bi, 'lru_lambda': lru_lambda,
        'conv_d_w': conv_d_w, 'conv_d_b': conv_d_b, 'ln_d_g': ln_d_g, 'ln_d_b': ln_d_b,
        'w_pa': w_pa, 'w_pb': w_pb, 'w_pc': w_pc, 'w_pd': w_pd, 'w_out': w_out,
        'norm2_g': norm2_g, 'peer_wq': peer_wq, 'peer_keys': peer_keys, 'peer_u': peer_u, 'peer_v': peer_v,
    }
    bp = x_prompt.shape[0]
    dt = x_prompt.dtype
    zero_bufs = (jnp.zeros((DEPTH, bp, CONV_A - 1, W_A), dt), jnp.zeros((DEPTH, bp, POOL_BUF, W_B), dt),
                 jnp.zeros((DEPTH, bp, CONV_C - 1, W_C), dt), jnp.zeros((DEPTH, bp, W_C), dt),
                 jnp.zeros((DEPTH, bp, CONV_D - 1, W_D), dt))
    y_prompt, (ca_p, pool_p, cc_p, h_p, cd_p) = run_group(x_prompt, zero_bufs, 0, layer_params, final_g)
    sample_bufs = (state_conv_a, state_pool, state_conv_c, state_lru_h, state_conv_d)
    y_sample, (ca_s, pool_s, cc_s, h_s, cd_s) = run_group(x_sample, sample_bufs, PAST_LEN, layer_params, final_g)
    return (y_prompt, y_sample, ca_p, ca_s, pool_p, pool_s, cc_p, cc_s, h_p, h_s, cd_p, cd_s)
```

```python
import functools

import jax
import jax.numpy as jnp
from jax import lax
from jax.experimental import pallas as pl
from jax.experimental.pallas import tpu as pltpu

F32 = jnp.float32
BF16 = jnp.bfloat16

D_MODEL = 1024
DEPTH = 2
PAST_LEN = 16384
W_A = 512
CONV_A = 3
W_B = 512
POOL_WINDOWS = (2, 4, 8, 16)
POOL_GROUP = 128
POOL_BUF = 15
W_C = 1024
LRU_HEADS = 8
LRU_HEAD_DIM = 128
CONV_C = 4
LRU_C = 8.0
W_D = 512
CONV_D = 31
OFF_B = 3 * W_A
OFF_C = OFF_B + W_B
OFF_D = OFF_C + 2 * W_C
OFF_G = OFF_D + 2 * W_D
IN_TOTAL = OFF_G + 4 * D_MODEL
PEER_HEADS = 8
N_KEYS = 128
N_EXPERTS = N_KEYS * N_KEYS
KEY_DIM = 128
PEER_TOPK = 16
EPS = 1e-6

LANES = 128
VMEM_LIMIT = 56 * 1024 * 1024

GELU_C0 = 0.7978845608028654
GELU_C1 = 0.044715
NEG_INF = float("-inf")


def _sigmoid(x):
    return 1.0 / (1.0 + jnp.exp(-x))


def _gelu(x):
    return 0.5 * x * (1.0 + jnp.tanh(GELU_C0 * (x + GELU_C1 * (x * x * x))))


def _rms(x, g):
    ms = jnp.mean(x * x, axis=-1, keepdims=True)
    return x * lax.rsqrt(ms + EPS) * g


def _bdot(a, b):
    return jnp.dot(a.astype(BF16), b.astype(BF16), preferred_element_type=F32)


def _inproj_kernel(x_ref, g_ref, w_ref, u_ref, xn_ref):
    @pl.when(pl.program_id(1) == 0)
    def _():
        xn_ref[...] = _rms(x_ref[...], g_ref[...]).astype(BF16)

    u_ref[...] = jnp.dot(xn_ref[...], w_ref[...], preferred_element_type=F32)


def _inproj(x, g, w_bf16, tm, tn):
    n = x.shape[0]
    return pl.pallas_call(
        _inproj_kernel,
        grid=(n // tm, IN_TOTAL // tn),
        in_specs=[
            pl.BlockSpec((tm, D_MODEL), lambda i, j: (i, 0)),
            pl.BlockSpec((1, D_MODEL), lambda i, j: (0, 0)),
            pl.BlockSpec((D_MODEL, tn), lambda i, j: (0, j)),
        ],
        out_specs=pl.BlockSpec((tm, tn), lambda i, j: (i, j)),
        out_shape=jax.ShapeDtypeStruct((n, IN_TOTAL), F32),
        scratch_shapes=[pltpu.VMEM((tm, D_MODEL), BF16)],
        compiler_params=pltpu.CompilerParams(
            dimension_semantics=("arbitrary", "arbitrary"), vmem_limit_bytes=VMEM_LIMIT),
        name="inproj",
    )(x, g, w_bf16)


def _mixer_kernel(nb, tt, carry, offset,
                  u_ref, x_ref, ha_ref, hb_ref, hc_ref, h0_ref, hd_ref,
                  caw_ref, pw_ref, ps_ref, ccw_ref, ccb_ref, wa_ref, ba_ref, wi_ref, bi_ref, lam_ref,
                  cdw_ref, cdb_ref, lng_ref, lnb_ref, bg_ref,
                  wpa_ref, wpb_ref, wpc_ref, wpd_ref, wout_ref,
                  xo_ref, sa_ref, sb_ref, sc_ref, hl_ref, sd_ref,
                  ea_ref, eb_ref, ec_ref, ed_ref, h_ref, a_ref, bx_ref, hs_ref, yb_ref, gate_ref):
    r = tt * nb
    step = pl.program_id(0)
    first = (step == 0) if carry else None

    def load_hist():
        ea_ref[0:(CONV_A - 1) * nb, :] = ha_ref[...]
        eb_ref[0:POOL_BUF * nb, :] = hb_ref[...]
        ec_ref[0:(CONV_C - 1) * nb, :] = hc_ref[...]
        ed_ref[0:(CONV_D - 1) * nb, :] = hd_ref[...]
        h_ref[...] = h0_ref[...]

    if carry:
        pl.when(first)(load_hist)
    else:
        load_hist()

    z = u_ref[:, W_A:2 * W_A] * u_ref[:, 2 * W_A:3 * W_A]
    ea_ref[(CONV_A - 1) * nb:, :] = z
    y = caw_ref[0:1, :] * ea_ref[0:r, :]
    for k in range(1, CONV_A):
        y = y + caw_ref[k:k + 1, :] * ea_ref[k * nb:k * nb + r, :]
    ya = u_ref[:, 0:W_A] * y

    eb_ref[POOL_BUF * nb:, :] = u_ref[:, OFF_B:OFF_C]
    row = lax.broadcasted_iota(jnp.int32, (r, POOL_GROUP), 0)
    tpos = (row >> (nb.bit_length() - 1)) + (step * tt if carry else 0) + offset
    for g, w in enumerate(POOL_WINDOWS):
        sl = slice(g * POOL_GROUP, (g + 1) * POOL_GROUP)
        acc = eb_ref[POOL_BUF * nb:POOL_BUF * nb + r, sl]
        for k in range(1, w):
            acc = acc + eb_ref[(POOL_BUF - k) * nb:(POOL_BUF - k) * nb + r, sl]
        cnt = jnp.minimum(tpos + 1, w).astype(F32)
        p = acc / cnt - eb_ref[POOL_BUF * nb:POOL_BUF * nb + r, sl]
        yb_ref[:, sl] = _bdot(p, pw_ref[g]) * ps_ref[:, sl]

    ec_ref[(CONV_C - 1) * nb:, :] = u_ref[:, OFF_C:OFF_C + W_C]
    xconv = ccw_ref[0:1, :] * ec_ref[0:r, :]
    for k in range(1, CONV_C):
        xconv = xconv + ccw_ref[k:k + 1, :] * ec_ref[k * nb:k * nb + r, :]
    xconv = xconv + ccb_ref[...]
    lam = lam_ref[...]
    nsp = -LRU_C * (jnp.maximum(-lam, 0.0) + jnp.log(1.0 + jnp.exp(-jnp.abs(lam))))
    for hh in range(LRU_HEADS):
        sl = slice(hh * LRU_HEAD_DIM, (hh + 1) * LRU_HEAD_DIM)
        xh = xconv[:, sl]
        xhb = xh.astype(BF16)
        rg = _sigmoid(jnp.dot(xhb, wa_ref[hh], preferred_element_type=F32) + ba_ref[:, sl])
        ig = _sigmoid(jnp.dot(xhb, wi_ref[hh], preferred_element_type=F32) + bi_ref[:, sl])
        log_a = rg * nsp[:, sl]
        a = jnp.exp(log_a)
        a_ref[:, sl] = a
        bx_ref[:, sl] = jnp.sqrt(1.0 - a * a) * (ig * xh)

    if tt <= 8:
        for t in range(tt):
            h = a_ref[t * nb:(t + 1) * nb, :] * h_ref[...] + bx_ref[t * nb:(t + 1) * nb, :]
            h_ref[...] = h
            hs_ref[t * nb:(t + 1) * nb, :] = h
    else:
        def scan_body(t, h):
            rows = pl.ds(pl.multiple_of(t * nb, nb), nb)
            h = a_ref[rows, :] * h + bx_ref[rows, :]
            hs_ref[rows, :] = h
            return h
        h_ref[...] = lax.fori_loop(0, tt, scan_body, h_ref[...])
    yc = hs_ref[...] * _gelu(u_ref[:, OFF_C + W_C:OFF_D])

    ed_ref[(CONV_D - 1) * nb:, :] = u_ref[:, OFF_D:OFF_D + W_D] * _sigmoid(u_ref[:, OFF_D + W_D:OFF_G])
    c = cdw_ref[0:1, :] * ed_ref[0:r, :]
    for k in range(1, CONV_D):
        c = c + cdw_ref[k:k + 1, :] * ed_ref[k * nb:k * nb + r, :]
    c = c + cdb_ref[...]
    mu = jnp.mean(c, axis=-1, keepdims=True)
    cc = c - mu
    var = jnp.mean(cc * cc, axis=-1, keepdims=True)
    ln = cc * lax.rsqrt(var + EPS) * lng_ref[...] + lnb_ref[...]
    yd = ln * _sigmoid(ln)

    gate_ref[...] = _sigmoid(u_ref[:, OFF_G:] + bg_ref[...])
    m = gate_ref[:, 0:D_MODEL] * _bdot(ya, wpa_ref[...])
    m = m + gate_ref[:, D_MODEL:2 * D_MODEL] * _bdot(yb_ref[...], wpb_ref[...])
    m = m + gate_ref[:, 2 * D_MODEL:3 * D_MODEL] * _bdot(yc, wpc_ref[...])
    m = m + gate_ref[:, 3 * D_MODEL:] * _bdot(yd, wpd_ref[...])
    xo_ref[...] = x_ref[...] + _bdot(m, wout_ref[...])

    def store_state():
        sa_ref[...] = ea_ref[r:r + (CONV_A - 1) * nb, :]
        sb_ref[...] = eb_ref[r:r + POOL_BUF * nb, :]
        sc_ref[...] = ec_ref[r:r + (CONV_C - 1) * nb, :]
        sd_ref[...] = ed_ref[r:r + (CONV_D - 1) * nb, :]
        hl_ref[...] = h_ref[...]

    if carry:
        pl.when(step == pl.num_programs(0) - 1)(store_state)
        for e_ref, hist in ((ea_ref, CONV_A - 1), (eb_ref, POOL_BUF), (ec_ref, CONV_C - 1), (ed_ref, CONV_D - 1)):
            total = hist * nb
            for s in range(0, total, r):
                n = min(r, total - s)
                e_ref[s:s + n, :] = e_ref[r + s:r + s + n, :]
    else:
        store_state()


def _mixer(u, x, hists, lp, nb, tt, carry, offset):
    n = x.shape[0]
    r = nb * tt
    steps = n // r
    ha, hb, hc, h0, hd = hists
    widths = (W_A, W_B, W_C, W_D)
    hist_rows = ((CONV_A - 1) * nb, POOL_BUF * nb, (CONV_C - 1) * nb, (CONV_D - 1) * nb)

    def chunk_map(i):
        return (0, 0) if carry else (i, 0)

    def const2(i):
        return (0, 0)

    def const3(i):
        return (0, 0, 0)

    def full(a):
        return pl.BlockSpec(a.shape, const2 if a.ndim == 2 else const3)

    params = [lp['conv_a_w'], lp['pool_w'], lp['pool_scale'], lp['conv_c_w'], lp['conv_c_b'],
              lp['lru_wa'], lp['lru_ba'], lp['lru_wi'], lp['lru_bi'], lp['lru_lambda'],
              lp['conv_d_w'], lp['conv_d_b'], lp['ln_d_g'], lp['ln_d_b'], lp['b_gate'],
              lp['w_pa'], lp['w_pb'], lp['w_pc'], lp['w_pd'], lp['w_out']]
    in_specs = [
        pl.BlockSpec((r, IN_TOTAL), lambda i: (i, 0)),
        pl.BlockSpec((r, D_MODEL), lambda i: (i, 0)),
        pl.BlockSpec((hist_rows[0], W_A), chunk_map),
        pl.BlockSpec((hist_rows[1], W_B), chunk_map),
        pl.BlockSpec((hist_rows[2], W_C), chunk_map),
        pl.BlockSpec((nb, W_C), chunk_map),
        pl.BlockSpec((hist_rows[3], W_D), chunk_map),
    ] + [full(p) for p in params]
    n_state = 1 if carry else steps
    out_shape = (
        jax.ShapeDtypeStruct((n, D_MODEL), F32),
        jax.ShapeDtypeStruct((n_state * hist_rows[0], W_A), F32),
        jax.ShapeDtypeStruct((n_state * hist_rows[1], W_B), F32),
        jax.ShapeDtypeStruct((n_state * hist_rows[2], W_C), F32),
        jax.ShapeDtypeStruct((n_state * nb, W_C), F32),
        jax.ShapeDtypeStruct((n_state * hist_rows[3], W_D), F32),
    )
    out_specs = (
        pl.BlockSpec((r, D_MODEL), lambda i: (i, 0)),
        pl.BlockSpec((hist_rows[0], W_A), chunk_map),
        pl.BlockSpec((hist_rows[1], W_B), chunk_map),
        pl.BlockSpec((hist_rows[2], W_C), chunk_map),
        pl.BlockSpec((nb, W_C), chunk_map),
        pl.BlockSpec((hist_rows[3], W_D), chunk_map),
    )
    scratch = [pltpu.VMEM((hr + r, w), F32) for hr, w in zip(hist_rows, widths)]
    scratch += [
        pltpu.VMEM((nb, W_C), F32),
        pltpu.VMEM((r, W_C), F32),
        pltpu.VMEM((r, W_C), F32),
        pltpu.VMEM((r, W_C), F32),
        pltpu.VMEM((r, W_B), F32),
        pltpu.VMEM((r, 4 * D_MODEL), F32),
    ]
    return pl.pallas_call(
        functools.partial(_mixer_kernel, nb, tt, carry, offset),
        grid=(steps,),
        in_specs=in_specs,
        out_specs=out_specs,
        out_shape=out_shape,
        scratch_shapes=scratch,
        compiler_params=pltpu.CompilerParams(
            dimension_semantics=("arbitrary",), vmem_limit_bytes=VMEM_LIMIT),
        name="mixer_carry" if carry else "mixer_batch",
    )(u, x, ha, hb, hc, h0, hd, *params)


N_TOP = PEER_TOPK + 1
SV_ROWS = 24


def _peer_kernel(tb, et, final,
                 x_ref, g_ref, wqt_ref, keys_ref, u_ref, vt_ref, fg_ref,
                 o_ref,
                 xnt_ref, s_ref, sv_ref, a_ref, b_ref, e2_ref, ht_ref, ct_ref, acc_ref):
    nlt = tb // LANES
    rows_per_step = et // N_KEYS
    j = pl.program_id(1)

    @pl.when(j == 0)
    def _():
        xn = _rms(x_ref[...], g_ref[...])
        xnt_ref[...] = xn.T.astype(BF16)
        qt = jnp.dot(wqt_ref[...], xnt_ref[...], preferred_element_type=F32)
        for hc in range(2 * PEER_HEADS):
            s = jnp.dot(keys_ref[hc], qt[hc * KEY_DIM:(hc + 1) * KEY_DIM, :].astype(BF16),
                        preferred_element_type=F32)
            for lt in range(nlt):
                s_ref[hc, lt] = s[:, lt * LANES:(lt + 1) * LANES]

        def top_body(idx, carry):
            hc = idx // nlt
            lt = idx % nlt
            s = s_ref[hc, lt]
            sv_ref[hc, lt] = jnp.full((SV_ROWS, LANES), NEG_INF, F32)
            for k in range(N_TOP):
                m = jnp.max(s, axis=0, keepdims=True)
                sv_ref[hc, lt, k:k + 1, :] = m
                s = jnp.where(s >= m, NEG_INF, s)
            return carry

        lax.fori_loop(0, 2 * PEER_HEADS * nlt, top_body, 0)

        def thr_body(idx, carry):
            h = idx // nlt
            lt = idx % nlt
            sv1 = sv_ref[2 * h, lt]
            sv2 = sv_ref[2 * h + 1, lt]
            pieces = [sv1[0:1, :] + sv2]
            for a in range(1, 8):
                pieces.append(sv1[a:a + 1, :] + sv2[0:8, :])
            pieces.append(sv1[8:SV_ROWS, :] + sv2[0:1, :])
            best = []
            for k in range(N_TOP):
                m = jnp.max(pieces[0], axis=0, keepdims=True)
                for p in pieces[1:]:
                    m = jnp.maximum(m, jnp.max(p, axis=0, keepdims=True))
                best.append(m)
                pieces = [jnp.where(p >= m, NEG_INF, p) for p in pieces]
            zsum = jnp.ones_like(best[0])
            for k in range(1, PEER_TOPK):
                zsum = zsum + jnp.exp(best[k] - best[0])
            thr = 0.5 * (best[PEER_TOPK - 1] + best[PEER_TOPK])
            s1 = s_ref[2 * h, lt]
            s2 = s_ref[2 * h + 1, lt]
            a_ref[h, lt] = jnp.exp(s1 - sv1[0:1, :]) / zsum
            b_ref[h, lt] = thr - s1
            e2_ref[h, lt] = jnp.exp(s2 - sv2[0:1, :])
            return carry

        lax.fori_loop(0, PEER_HEADS * nlt, thr_body, 0)
        acc_ref[...] = jnp.zeros_like(acc_ref)

    ht_ref[...] = jnp.dot(u_ref[...], xnt_ref[...], preferred_element_type=F32)
    for rr in range(rows_per_step):
        i1 = j * rows_per_step + rr
        for lt in range(nlt):
            g = jnp.zeros((N_KEYS, LANES), F32)
            for h in range(PEER_HEADS):
                a_row = a_ref[h, lt, pl.ds(i1, 1), :]
                b_row = b_ref[h, lt, pl.ds(i1, 1), :]
                g = g + jnp.where(s_ref[2 * h + 1, lt] >= b_row, e2_ref[h, lt], 0.0) * a_row
            hv = ht_ref[rr * N_KEYS:(rr + 1) * N_KEYS, lt * LANES:(lt + 1) * LANES]
            ct_ref[rr * N_KEYS:(rr + 1) * N_KEYS, lt * LANES:(lt + 1) * LANES] = (g * _gelu(hv)).astype(BF16)
    acc_ref[...] += jnp.dot(vt_ref[...], ct_ref[...], preferred_element_type=F32)

    @pl.when(j == pl.num_programs(1) - 1)
    def _():
        y = x_ref[...] + acc_ref[...].T
        if final:
            y = _rms(y, fg_ref[...])
        o_ref[...] = y


def _peer(x, g, wqt, keys, u_tab, vt_tab, final_g, final, tb, et):
    n = x.shape[0]
    nlt = tb // LANES
    return pl.pallas_call(
        functools.partial(_peer_kernel, tb, et, final),
        grid=(n // tb, N_EXPERTS // et),
        in_specs=[
            pl.BlockSpec((tb, D_MODEL), lambda i, j: (i, 0)),
            pl.BlockSpec((1, D_MODEL), lambda i, j: (0, 0)),
            pl.BlockSpec(wqt.shape, lambda i, j: (0, 0)),
            pl.BlockSpec(keys.shape, lambda i, j: (0, 0, 0)),
            pl.BlockSpec((et, D_MODEL), lambda i, j: (j, 0)),
            pl.BlockSpec((D_MODEL, et), lambda i, j: (0, j)),
            pl.BlockSpec((1, D_MODEL), lambda i, j: (0, 0)),
        ],
        out_specs=pl.BlockSpec((tb, D_MODEL), lambda i, j: (i, 0)),
        out_shape=jax.ShapeDtypeStruct((n, D_MODEL), F32),
        scratch_shapes=[
            pltpu.VMEM((D_MODEL, tb), BF16),
            pltpu.VMEM((2 * PEER_HEADS, nlt, N_KEYS, LANES), F32),
            pltpu.VMEM((2 * PEER_HEADS, nlt, SV_ROWS, LANES), F32),
            pltpu.VMEM((PEER_HEADS, nlt, N_KEYS, LANES), F32),
            pltpu.VMEM((PEER_HEADS, nlt, N_KEYS, LANES), F32),
            pltpu.VMEM((PEER_HEADS, nlt, N_KEYS, LANES), F32),
            pltpu.VMEM((et, tb), F32),
            pltpu.VMEM((et, tb), BF16),
            pltpu.VMEM((D_MODEL, tb), F32),
        ],
        compiler_params=pltpu.CompilerParams(
            dimension_semantics=("arbitrary", "arbitrary"), vmem_limit_bytes=VMEM_LIMIT),
        name="peer_final" if final else "peer",
    )(x, g, wqt, keys, u_tab, vt_tab, final_g)


PROMPT_NB = 8
PROMPT_TT = 32
SAMPLE_NB = 32


def _to_chunks(a, nb):
    b, t, c = a.shape
    return a.reshape(b // nb, nb, t, c).transpose(0, 2, 1, 3).reshape(b * t, c)


def _from_chunks(a, nb, b, t):
    c = a.shape[-1]
    return a.reshape(b // nb, t, nb, c).transpose(0, 2, 1, 3).reshape(b, t, c)


def kernel(x_prompt, x_sample, state_conv_a, state_pool, state_conv_c, state_lru_h, state_conv_d, norm1_g, w_in, b_gate, conv_a_w, pool_w, pool_scale, conv_c_w, conv_c_b, lru_wa, lru_ba, lru_wi, lru_bi, lru_lambda, conv_d_w, conv_d_b, ln_d_g, ln_d_b, w_pa, w_pb, w_pc, w_pd, w_out, norm2_g, peer_wq, peer_keys, peer_u, peer_v, final_g):
    bp, tp, _ = x_prompt.shape
    bs, ts, _ = x_sample.shape
    xp = _to_chunks(x_prompt, PROMPT_NB)
    xs = _to_chunks(x_sample, SAMPLE_NB)
    fg = final_g.reshape(1, D_MODEL)

    outs_p = [[] for _ in range(5)]
    outs_s = [[] for _ in range(5)]
    for l in range(DEPTH):
        lp = {
            'conv_a_w': conv_a_w[l], 'pool_w': pool_w[l].astype(BF16), 'pool_scale': pool_scale[l].reshape(1, W_B),
            'conv_c_w': conv_c_w[l], 'conv_c_b': conv_c_b[l].reshape(1, W_C),
            'lru_wa': lru_wa[l].astype(BF16), 'lru_ba': lru_ba[l].reshape(1, W_C),
            'lru_wi': lru_wi[l].astype(BF16), 'lru_bi': lru_bi[l].reshape(1, W_C),
            'lru_lambda': lru_lambda[l].reshape(1, W_C),
            'conv_d_w': conv_d_w[l], 'conv_d_b': conv_d_b[l].reshape(1, W_D),
            'ln_d_g': ln_d_g[l].reshape(1, W_D), 'ln_d_b': ln_d_b[l].reshape(1, W_D),
            'b_gate': b_gate[l].reshape(1, 4 * D_MODEL),
            'w_pa': w_pa[l].astype(BF16), 'w_pb': w_pb[l].astype(BF16), 'w_pc': w_pc[l].astype(BF16),
            'w_pd': w_pd[l].astype(BF16), 'w_out': w_out[l].astype(BF16),
        }
        g1 = norm1_g[l].reshape(1, D_MODEL)
        g2 = norm2_g[l].reshape(1, D_MODEL)
        w_in_b = w_in[l].astype(BF16)
        wqt = peer_wq[l].T.astype(BF16)
        keys = peer_keys[l].reshape(2 * PEER_HEADS, N_KEYS, KEY_DIM).astype(BF16)
        u_tab = peer_u[l].astype(BF16)
        vt_tab = peer_v[l].T.astype(BF16)
        final = l == DEPTH - 1

        zero_hist = (jnp.zeros(((CONV_A - 1) * bp, W_A), F32), jnp.zeros((POOL_BUF * bp, W_B), F32),
                     jnp.zeros(((CONV_C - 1) * bp, W_C), F32), jnp.zeros((bp, W_C), F32),
                     jnp.zeros(((CONV_D - 1) * bp, W_D), F32))
        u = _inproj(xp, g1, w_in_b, 1024, 1536)
        xp, *st = _mixer(u, xp, zero_hist, lp, PROMPT_NB, PROMPT_TT, True, 0)
        for acc, s in zip(outs_p, st):
            acc.append(s)
        xp = _peer(xp, g2, wqt, keys, u_tab, vt_tab, fg, final, 512, 512)

        hist = (_to_chunks(state_conv_a[l], SAMPLE_NB), _to_chunks(state_pool[l], SAMPLE_NB),
                _to_chunks(state_conv_c[l], SAMPLE_NB), state_lru_h[l], _to_chunks(state_conv_d[l], SAMPLE_NB))
        u = _inproj(xs, g1, w_in_b, 512, 1536)
        xs, *st = _mixer(u, xs, hist, lp, SAMPLE_NB, ts, False, PAST_LEN)
        for acc, s in zip(outs_s, st):
            acc.append(s)
        xs = _peer(xs, g2, wqt, keys, u_tab, vt_tab, fg, final, 512, 512)

    y_prompt = _from_chunks(xp, PROMPT_NB, bp, tp)
    y_sample = _from_chunks(xs, SAMPLE_NB, bs, ts)

    def states(outs, nb, b):
        res = []
        for idx, hist_len in ((0, CONV_A - 1), (1, POOL_BUF), (2, CONV_C - 1), (3, None), (4, CONV_D - 1)):
            if hist_len is None:
                res.append(jnp.stack(outs[idx]))
            else:
                res.append(jnp.stack([_from_chunks(s, nb, b, hist_len) for s in outs[idx]]))
        return res

    ca_p, pool_p, cc_p, h_p, cd_p = states(outs_p, PROMPT_NB, bp)
    ca_s, pool_s, cc_s, h_s, cd_s = states(outs_s, SAMPLE_NB, bs)
    return (y_prompt, y_sample, ca_p, ca_s, pool_p, pool_s, cc_p, cc_s, h_p, h_s, cd_p, cd_s)
```

```python
import functools

import jax
import jax.numpy as jnp
from jax import lax
from jax.experimental import pallas as pl
from jax.experimental.pallas import tpu as pltpu

F32 = jnp.float32
BF16 = jnp.bfloat16

D_MODEL = 1024
DEPTH = 2
PAST_LEN = 16384
W_A = 512
CONV_A = 3
W_B = 512
POOL_WINDOWS = (2, 4, 8, 16)
POOL_GROUP = 128
POOL_BUF = 15
W_C = 1024
LRU_HEADS = 8
LRU_HEAD_DIM = 128
CONV_C = 4
LRU_C = 8.0
W_D = 512
CONV_D = 31
OFF_B = 3 * W_A
OFF_C = OFF_B + W_B
OFF_D = OFF_C + 2 * W_C
OFF_G = OFF_D + 2 * W_D
IN_TOTAL = OFF_G + 4 * D_MODEL
PEER_HEADS = 8
N_KEYS = 128
N_EXPERTS = N_KEYS * N_KEYS
KEY_DIM = 128
PEER_TOPK = 16
EPS = 1e-6

LANES = 128
VMEM_LIMIT = 56 * 1024 * 1024

GELU_C0 = 0.7978845608028654
GELU_C1 = 0.044715
NEG_INF = float("-inf")


def _sigmoid(x):
    return 1.0 / (1.0 + jnp.exp(-x))


def _gelu(x):
    return 0.5 * x * (1.0 + jnp.tanh(GELU_C0 * (x + GELU_C1 * (x * x * x))))


def _rms(x, g):
    ms = jnp.mean(x * x, axis=-1, keepdims=True)
    return x * lax.rsqrt(ms + EPS) * g


def _bdot(a, b):
    return jnp.dot(a.astype(BF16), b.astype(BF16), preferred_element_type=F32)


def _inproj_kernel(x_ref, g_ref, w_ref, u_ref, xn_ref):
    @pl.when(pl.program_id(1) == 0)
    def _():
        xn_ref[...] = _rms(x_ref[...], g_ref[...]).astype(BF16)

    u_ref[...] = jnp.dot(xn_ref[...], w_ref[...], preferred_element_type=F32)


def _inproj(x, g, w_bf16, tm, tn):
    n = x.shape[0]
    return pl.pallas_call(
        _inproj_kernel,
        grid=(n // tm, IN_TOTAL // tn),
        in_specs=[
            pl.BlockSpec((tm, D_MODEL), lambda i, j: (i, 0)),
            pl.BlockSpec((1, D_MODEL), lambda i, j: (0, 0)),
            pl.BlockSpec((D_MODEL, tn), lambda i, j: (0, j)),
        ],
        out_specs=pl.BlockSpec((tm, tn), lambda i, j: (i, j)),
        out_shape=jax.ShapeDtypeStruct((n, IN_TOTAL), F32),
        scratch_shapes=[pltpu.VMEM((tm, D_MODEL), BF16)],
        compiler_params=pltpu.CompilerParams(
            dimension_semantics=("arbitrary", "arbitrary"), vmem_limit_bytes=VMEM_LIMIT),
        name="inproj",
    )(x, g, w_bf16)


def _mixer_kernel(nb, tt, carry, offset,
                  u_ref, x_ref, ha_ref, hb_ref, hc_ref, h0_ref, hd_ref,
                  caw_ref, pw_ref, ps_ref, ccw_ref, ccb_ref, wa_ref, ba_ref, wi_ref, bi_ref, lam_ref,
                  cdw_ref, cdb_ref, lng_ref, lnb_ref, bg_ref,
                  wpa_ref, wpb_ref, wpc_ref, wpd_ref, wout_ref,
                  xo_ref, sa_ref, sb_ref, sc_ref, hl_ref, sd_ref,
                  ea_ref, eb_ref, ec_ref, ed_ref, h_ref, a_ref, bx_ref, hs_ref, yb_ref, gate_ref):
    r = tt * nb
    step = pl.program_id(0)
    first = (step == 0) if carry else None

    def load_hist():
        ea_ref[0:(CONV_A - 1) * nb, :] = ha_ref[...]
        eb_ref[0:POOL_BUF * nb, :] = hb_ref[...]
        ec_ref[0:(CONV_C - 1) * nb, :] = hc_ref[...]
        ed_ref[0:(CONV_D - 1) * nb, :] = hd_ref[...]
        h_ref[...] = h0_ref[...]

    if carry:
        pl.when(first)(load_hist)
    else:
        load_hist()

    z = u_ref[:, W_A:2 * W_A] * u_ref[:, 2 * W_A:3 * W_A]
    ea_ref[(CONV_A - 1) * nb:, :] = z
    y = caw_ref[0:1, :] * ea_ref[0:r, :]
    for k in range(1, CONV_A):
        y = y + caw_ref[k:k + 1, :] * ea_ref[k * nb:k * nb + r, :]
    ya = u_ref[:, 0:W_A] * y

    eb_ref[POOL_BUF * nb:, :] = u_ref[:, OFF_B:OFF_C]
    row = lax.broadcasted_iota(jnp.int32, (r, POOL_GROUP), 0)
    tpos = (row >> (nb.bit_length() - 1)) + (step * tt if carry else 0) + offset
    for g, w in enumerate(POOL_WINDOWS):
        sl = slice(g * POOL_GROUP, (g + 1) * POOL_GROUP)
        acc = eb_ref[POOL_BUF * nb:POOL_BUF * nb + r, sl]
        for k in range(1, w):
            acc = acc + eb_ref[(POOL_BUF - k) * nb:(POOL_BUF - k) * nb + r, sl]
        cnt = jnp.minimum(tpos + 1, w).astype(F32)
        p = acc / cnt - eb_ref[POOL_BUF * nb:POOL_BUF * nb + r, sl]
        yb_ref[:, sl] = _bdot(p, pw_ref[g]) * ps_ref[:, sl]

    ec_ref[(CONV_C - 1) * nb:, :] = u_ref[:, OFF_C:OFF_C + W_C]
    xconv = ccw_ref[0:1, :] * ec_ref[0:r, :]
    for k in range(1, CONV_C):
        xconv = xconv + ccw_ref[k:k + 1, :] * ec_ref[k * nb:k * nb + r, :]
    xconv = xconv + ccb_ref[...]
    lam = lam_ref[...]
    nsp = -LRU_C * (jnp.maximum(-lam, 0.0) + jnp.log(1.0 + jnp.exp(-jnp.abs(lam))))
    for hh in range(LRU_HEADS):
        sl = slice(hh * LRU_HEAD_DIM, (hh + 1) * LRU_HEAD_DIM)
        xh = xconv[:, sl]
        xhb = xh.astype(BF16)
        rg = _sigmoid(jnp.dot(xhb, wa_ref[hh], preferred_element_type=F32) + ba_ref[:, sl])
        ig = _sigmoid(jnp.dot(xhb, wi_ref[hh], preferred_element_type=F32) + bi_ref[:, sl])
        log_a = rg * nsp[:, sl]
        a = jnp.exp(log_a)
        a_ref[:, sl] = a
        bx_ref[:, sl] = jnp.sqrt(1.0 - a * a) * (ig * xh)

    if tt <= 8:
        for t in range(tt):
            h = a_ref[t * nb:(t + 1) * nb, :] * h_ref[...] + bx_ref[t * nb:(t + 1) * nb, :]
            h_ref[...] = h
            hs_ref[t * nb:(t + 1) * nb, :] = h
    else:
        def scan_body(t, h):
            rows = pl.ds(pl.multiple_of(t * nb, nb), nb)
            h = a_ref[rows, :] * h + bx_ref[rows, :]
            hs_ref[rows, :] = h
            return h
        h_ref[...] = lax.fori_loop(0, tt, scan_body, h_ref[...])
    yc = hs_ref[...] * _gelu(u_ref[:, OFF_C + W_C:OFF_D])

    ed_ref[(CONV_D - 1) * nb:, :] = u_ref[:, OFF_D:OFF_D + W_D] * _sigmoid(u_ref[:, OFF_D + W_D:OFF_G])
    c = cdw_ref[0:1, :] * ed_ref[0:r, :]
    for k in range(1, CONV_D):
        c = c + cdw_ref[k:k + 1, :] * ed_ref[k * nb:k * nb + r, :]
    c = c + cdb_ref[...]
    mu = jnp.mean(c, axis=-1, keepdims=True)
    cc = c - mu
    var = jnp.mean(cc * cc, axis=-1, keepdims=True)
    ln = cc * lax.rsqrt(var + EPS) * lng_ref[...] + lnb_ref[...]
    yd = ln * _sigmoid(ln)

    gate_ref[...] = _sigmoid(u_ref[:, OFF_G:] + bg_ref[...])
    m = gate_ref[:, 0:D_MODEL] * _bdot(ya, wpa_ref[...])
    m = m + gate_ref[:, D_MODEL:2 * D_MODEL] * _bdot(yb_ref[...], wpb_ref[...])
    m = m + gate_ref[:, 2 * D_MODEL:3 * D_MODEL] * _bdot(yc, wpc_ref[...])
    m = m + gate_ref[:, 3 * D_MODEL:] * _bdot(yd, wpd_ref[...])
    xo_ref[...] = x_ref[...] + _bdot(m, wout_ref[...])

    def store_state():
        sa_ref[...] = ea_ref[r:r + (CONV_A - 1) * nb, :]
        sb_ref[...] = eb_ref[r:r + POOL_BUF * nb, :]
        sc_ref[...] = ec_ref[r:r + (CONV_C - 1) * nb, :]
        sd_ref[...] = ed_ref[r:r + (CONV_D - 1) * nb, :]
        hl_ref[...] = h_ref[...]

    if carry:
        pl.when(step == pl.num_programs(0) - 1)(store_state)
        for e_ref, hist in ((ea_ref, CONV_A - 1), (eb_ref, POOL_BUF), (ec_ref, CONV_C - 1), (ed_ref, CONV_D - 1)):
            total = hist * nb
            for s in range(0, total, r):
                n = min(r, total - s)
                e_ref[s:s + n, :] = e_ref[r + s:r + s + n, :]
    else:
        store_state()


def _mixer(u, x, hists, lp, nb, tt, carry, offset):
    n = x.shape[0]
    r = nb * tt
    steps = n // r
    ha, hb, hc, h0, hd = hists
    widths = (W_A, W_B, W_C, W_D)
    hist_rows = ((CONV_A - 1) * nb, POOL_BUF * nb, (CONV_C - 1) * nb, (CONV_D - 1) * nb)

    def chunk_map(i):
        return (0, 0) if carry else (i, 0)

    def const2(i):
        return (0, 0)

    def const3(i):
        return (0, 0, 0)

    def full(a):
        return pl.BlockSpec(a.shape, const2 if a.ndim == 2 else const3)

    params = [lp['conv_a_w'], lp['pool_w'], lp['pool_scale'], lp['conv_c_w'], lp['conv_c_b'],
              lp['lru_wa'], lp['lru_ba'], lp['lru_wi'], lp['lru_bi'], lp['lru_lambda'],
              lp['conv_d_w'], lp['conv_d_b'], lp['ln_d_g'], lp['ln_d_b'], lp['b_gate'],
              lp['w_pa'], lp['w_pb'], lp['w_pc'], lp['w_pd'], lp['w_out']]
    in_specs = [
        pl.BlockSpec((r, IN_TOTAL), lambda i: (i, 0)),
        pl.BlockSpec((r, D_MODEL), lambda i: (i, 0)),
        pl.BlockSpec((hist_rows[0], W_A), chunk_map),
        pl.BlockSpec((hist_rows[1], W_B), chunk_map),
        pl.BlockSpec((hist_rows[2], W_C), chunk_map),
        pl.BlockSpec((nb, W_C), chunk_map),
        pl.BlockSpec((hist_rows[3], W_D), chunk_map),
    ] + [full(p) for p in params]
    n_state = 1 if carry else steps
    out_shape = (
        jax.ShapeDtypeStruct((n, D_MODEL), F32),
        jax.ShapeDtypeStruct((n_state * hist_rows[0], W_A), F32),
        jax.ShapeDtypeStruct((n_state * hist_rows[1], W_B), F32),
        jax.ShapeDtypeStruct((n_state * hist_rows[2], W_C), F32),
        jax.ShapeDtypeStruct((n_state * nb, W_C), F32),
        jax.ShapeDtypeStruct((n_state * hist_rows[3], W_D), F32),
    )
    out_specs = (
        pl.BlockSpec((r, D_MODEL), lambda i: (i, 0)),
        pl.BlockSpec((hist_rows[0], W_A), chunk_map),
        pl.BlockSpec((hist_rows[1], W_B), chunk_map),
        pl.BlockSpec((hist_rows[2], W_C), chunk_map),
        pl.BlockSpec((nb, W_C), chunk_map),
        pl.BlockSpec((hist_rows[3], W_D), chunk_map),
    )
    scratch = [pltpu.VMEM((hr + r, w), F32) for hr, w in zip(hist_rows, widths)]
    scratch += [
        pltpu.VMEM((nb, W_C), F32),
        pltpu.VMEM((r, W_C), F32),
        pltpu.VMEM((r, W_C), F32),
        pltpu.VMEM((r, W_C), F32),
        pltpu.VMEM((r, W_B), F32),
        pltpu.VMEM((r, 4 * D_MODEL), F32),
    ]
    return pl.pallas_call(
        functools.partial(_mixer_kernel, nb, tt, carry, offset),
        grid=(steps,),
        in_specs=in_specs,
        out_specs=out_specs,
        out_shape=out_shape,
        scratch_shapes=scratch,
        compiler_params=pltpu.CompilerParams(
            dimension_semantics=("arbitrary",), vmem_limit_bytes=VMEM_LIMIT),
        name="mixer_carry" if carry else "mixer_batch",
    )(u, x, ha, hb, hc, h0, hd, *params)


N_TOP = PEER_TOPK + 1
SV_ROWS = 24


PRO_LANES = 2 * LANES
SUBLANES = 8
MXU_COLS = 256


def _batcher_pairs(n):
    pairs = []

    def merge(lo, hi, r):
        step = r * 2
        if step < hi - lo:
            merge(lo, hi, step)
            merge(lo + r, hi, step)
            pairs.extend((i, i + r) for i in range(lo + r, hi - r, step))
        else:
            pairs.append((lo, lo + r))

    def sort(lo, hi):
        if hi - lo >= 1:
            mid = lo + (hi - lo) // 2
            sort(lo, mid)
            sort(mid + 1, hi)
            merge(lo, hi, 1)

    sort(0, n - 1)
    return pairs


def _sort_desc(ws):
    n = 1
    while n < len(ws):
        n *= 2
    ws = list(ws) + [None] * (n - len(ws))
    for i, j in _batcher_pairs(n):
        a, b = ws[i], ws[j]
        if b is None:
            continue
        if a is None:
            ws[i], ws[j] = b, None
        else:
            ws[i], ws[j] = jnp.maximum(a, b), jnp.minimum(a, b)
    return [w for w in ws if w is not None]


def _top_rows(ws, n):
    out = []
    for k in range(n):
        m = jnp.max(ws[0], axis=0, keepdims=True)
        out.append(m)
        remaining = n - 1 - k
        if remaining == 0:
            break
        hit = ws[0] >= m
        ws = [jnp.where(hit, ws[i + 1] if i + 1 < len(ws) else NEG_INF, ws[i])
              for i in range(min(len(ws), remaining))]
    return out


def _peer_kernel(tb, et, ts, final,
                 x_ref, g_ref, wqt_ref, keys_ref, u_ref, vt_ref, fg_ref,
                 o_ref,
                 xnt_ref, qt_ref, s_ref, sv_ref, a_ref, b_ref, e2_ref, ht_ref, ct_ref, acc_ref):
    npl = tb // PRO_LANES
    rows_per_tile = et // N_KEYS
    j = pl.program_id(1)

    @pl.when(j == 0)
    def _():
        xn = _rms(x_ref[...], g_ref[...])
        xnt_ref[...] = xn.T.astype(BF16)
        qt_ref[...] = jnp.dot(wqt_ref[...], xnt_ref[...], preferred_element_type=F32).astype(BF16)
        for hc in range(2 * PEER_HEADS):
            s = jnp.dot(keys_ref[hc], qt_ref[hc * KEY_DIM:(hc + 1) * KEY_DIM, :],
                        preferred_element_type=F32)
            for pt in range(npl):
                s_ref[hc, pt] = s[:, pt * PRO_LANES:(pt + 1) * PRO_LANES]

        def top_body(idx, carry):
            hc = idx // npl
            pt = idx % npl
            ws = _sort_desc([s_ref[hc, pt, i * SUBLANES:(i + 1) * SUBLANES, :] for i in range(N_KEYS // SUBLANES)])
            rows = _top_rows(ws, N_TOP)
            sv_ref[hc, pt] = jnp.full((SV_ROWS, PRO_LANES), NEG_INF, F32)
            for k in range(N_TOP):
                sv_ref[hc, pt, k:k + 1, :] = rows[k]
            return carry

        lax.fori_loop(0, 2 * PEER_HEADS * npl, top_body, 0)

        def thr_body(idx, carry):
            h = idx // npl
            pt = idx % npl
            sv1 = sv_ref[2 * h, pt]
            sv2 = sv_ref[2 * h + 1, pt]
            pieces = [sv1[0:1, :] + sv2[i * SUBLANES:(i + 1) * SUBLANES, :] for i in range(SV_ROWS // SUBLANES)]
            pieces += [sv1[a:a + 1, :] + sv2[0:SUBLANES, :] for a in range(1, SUBLANES)]
            pieces += [sv1[i * SUBLANES:(i + 1) * SUBLANES, :] + sv2[0:1, :] for i in range(1, SV_ROWS // SUBLANES)]
            best = _top_rows(_sort_desc(pieces), N_TOP)
            zsum = jnp.ones_like(best[0])
            for k in range(1, PEER_TOPK):
                zsum = zsum + jnp.exp(best[k] - best[0])
            thr = 0.5 * (best[PEER_TOPK - 1] + best[PEER_TOPK])
            s1 = s_ref[2 * h, pt]
            s2 = s_ref[2 * h + 1, pt]
            a_ref[h, pt] = jnp.exp(s1 - sv1[0:1, :]) / zsum
            b_ref[h, pt] = thr - s1
            e2_ref[h, pt] = jnp.exp(s2 - sv2[0:1, :])
            return carry

        lax.fori_loop(0, PEER_HEADS * npl, thr_body, 0)
        acc_ref[...] = jnp.zeros_like(acc_ref)

    pieces = [(t, c) for t in range(ts) for c in range(tb // MXU_COLS)]
    units = [(t, c, rr) for (t, c) in pieces for rr in range(rows_per_tile)]
    out_rows = D_MODEL // rows_per_tile

    def pre_activations(t, c, rr):
        er = slice(t * et + rr * N_KEYS, t * et + (rr + 1) * N_KEYS)
        cols = slice(c * MXU_COLS, (c + 1) * MXU_COLS)
        ht_ref[er, cols] = jnp.dot(u_ref[er, :], xnt_ref[:, cols], preferred_element_type=F32)

    def output_term(t, c, m):
        rows = slice(t * et, (t + 1) * et)
        cols = slice(c * MXU_COLS, (c + 1) * MXU_COLS)
        mr = slice(m * out_rows, (m + 1) * out_rows)
        acc_ref[mr, cols] += jnp.dot(vt_ref[t, mr, :], ct_ref[rows, cols], preferred_element_type=F32)

    def coefficient_tile(t, c, rr, l):
        i1 = (j * ts + t) * rows_per_tile + rr
        lt = c * (MXU_COLS // LANES) + l
        pt = lt // 2
        ln = slice((lt % 2) * LANES, (lt % 2 + 1) * LANES)
        g = jnp.zeros((N_KEYS, LANES), F32)
        for h in range(PEER_HEADS):
            a_row = a_ref[h, pt, pl.ds(i1, 1), :][:, ln]
            b_row = b_ref[h, pt, pl.ds(i1, 1), :][:, ln]
            g = g + jnp.where(s_ref[2 * h + 1, pt, :, ln] >= b_row, e2_ref[h, pt, :, ln], 0.0) * a_row
        er = slice(t * et + rr * N_KEYS, t * et + (rr + 1) * N_KEYS)
        tl = slice(lt * LANES, (lt + 1) * LANES)
        ct_ref[er, tl] = (g * _gelu(ht_ref[er, tl])).astype(BF16)

    pre_activations(*units[0])
    for n, (t, c, rr) in enumerate(units):
        if n + 1 < len(units):
            pre_activations(*units[n + 1])
        coefficient_tile(t, c, rr, 0)
        if n >= rows_per_tile:
            pt_, pc_, _ = units[n - rows_per_tile]
            output_term(pt_, pc_, rr)
        coefficient_tile(t, c, rr, 1)
    for m in range(rows_per_tile):
        output_term(*pieces[-1], m)

    @pl.when(j == pl.num_programs(1) - 1)
    def _():
        y = x_ref[...] + acc_ref[...].T
        if final:
            y = _rms(y, fg_ref[...])
        o_ref[...] = y


def _peer(x, g, wqt, keys, u_tab, vt_tiles, final_g, final, tb, et, ts):
    n = x.shape[0]
    npl = tb // PRO_LANES
    return pl.pallas_call(
        functools.partial(_peer_kernel, tb, et, ts, final),
        grid=(n // tb, N_EXPERTS // (ts * et)),
        in_specs=[
            pl.BlockSpec((tb, D_MODEL), lambda i, j: (i, 0)),
            pl.BlockSpec((1, D_MODEL), lambda i, j: (0, 0)),
            pl.BlockSpec(wqt.shape, lambda i, j: (0, 0)),
            pl.BlockSpec(keys.shape, lambda i, j: (0, 0, 0)),
            pl.BlockSpec((ts * et, D_MODEL), lambda i, j: (j, 0)),
            pl.BlockSpec((ts, D_MODEL, et), lambda i, j: (j, 0, 0)),
            pl.BlockSpec((1, D_MODEL), lambda i, j: (0, 0)),
        ],
        out_specs=pl.BlockSpec((tb, D_MODEL), lambda i, j: (i, 0)),
        out_shape=jax.ShapeDtypeStruct((n, D_MODEL), F32),
        scratch_shapes=[
            pltpu.VMEM((D_MODEL, tb), BF16),
            pltpu.VMEM((2 * PEER_HEADS * KEY_DIM, tb), BF16),
            pltpu.VMEM((2 * PEER_HEADS, npl, N_KEYS, PRO_LANES), F32),
            pltpu.VMEM((2 * PEER_HEADS, npl, SV_ROWS, PRO_LANES), F32),
            pltpu.VMEM((PEER_HEADS, npl, N_KEYS, PRO_LANES), F32),
            pltpu.VMEM((PEER_HEADS, npl, N_KEYS, PRO_LANES), F32),
            pltpu.VMEM((PEER_HEADS, npl, N_KEYS, PRO_LANES), F32),
            pltpu.VMEM((ts * et, tb), F32),
            pltpu.VMEM((ts * et, tb), BF16),
            pltpu.VMEM((D_MODEL, tb), F32),
        ],
        compiler_params=pltpu.CompilerParams(
            dimension_semantics=("arbitrary", "arbitrary"), vmem_limit_bytes=VMEM_LIMIT),
        name="peer_final" if final else "peer",
    )(x, g, wqt, keys, u_tab, vt_tiles, final_g)


PROMPT_NB = 8
PROMPT_TT = 32
SAMPLE_NB = 32
PEER_TB = 512
PEER_ET = 512
PEER_TS = 4


def _to_chunks(a, nb):
    b, t, c = a.shape
    return a.reshape(b // nb, nb, t, c).transpose(0, 2, 1, 3).reshape(b * t, c)


def _from_chunks(a, nb, b, t):
    c = a.shape[-1]
    return a.reshape(b // nb, t, nb, c).transpose(0, 2, 1, 3).reshape(b, t, c)


def kernel(x_prompt, x_sample, state_conv_a, state_pool, state_conv_c, state_lru_h, state_conv_d, norm1_g, w_in, b_gate, conv_a_w, pool_w, pool_scale, conv_c_w, conv_c_b, lru_wa, lru_ba, lru_wi, lru_bi, lru_lambda, conv_d_w, conv_d_b, ln_d_g, ln_d_b, w_pa, w_pb, w_pc, w_pd, w_out, norm2_g, peer_wq, peer_keys, peer_u, peer_v, final_g):
    bp, tp, _ = x_prompt.shape
    bs, ts, _ = x_sample.shape
    xp = _to_chunks(x_prompt, PROMPT_NB)
    xs = _to_chunks(x_sample, SAMPLE_NB)
    fg = final_g.reshape(1, D_MODEL)

    outs_p = [[] for _ in range(5)]
    outs_s = [[] for _ in range(5)]
    for l in range(DEPTH):
        lp = {
            'conv_a_w': conv_a_w[l], 'pool_w': pool_w[l].astype(BF16), 'pool_scale': pool_scale[l].reshape(1, W_B),
            'conv_c_w': conv_c_w[l], 'conv_c_b': conv_c_b[l].reshape(1, W_C),
            'lru_wa': lru_wa[l].astype(BF16), 'lru_ba': lru_ba[l].reshape(1, W_C),
            'lru_wi': lru_wi[l].astype(BF16), 'lru_bi': lru_bi[l].reshape(1, W_C),
            'lru_lambda': lru_lambda[l].reshape(1, W_C),
            'conv_d_w': conv_d_w[l], 'conv_d_b': conv_d_b[l].reshape(1, W_D),
            'ln_d_g': ln_d_g[l].reshape(1, W_D), 'ln_d_b': ln_d_b[l].reshape(1, W_D),
            'b_gate': b_gate[l].reshape(1, 4 * D_MODEL),
            'w_pa': w_pa[l].astype(BF16), 'w_pb': w_pb[l].astype(BF16), 'w_pc': w_pc[l].astype(BF16),
            'w_pd': w_pd[l].astype(BF16), 'w_out': w_out[l].astype(BF16),
        }
        g1 = norm1_g[l].reshape(1, D_MODEL)
        g2 = norm2_g[l].reshape(1, D_MODEL)
        w_in_b = w_in[l].astype(BF16)
        wqt = peer_wq[l].T.astype(BF16)
        keys = peer_keys[l].reshape(2 * PEER_HEADS, N_KEYS, KEY_DIM).astype(BF16)
        u_tab = peer_u[l].astype(BF16)
        vt_tab = peer_v[l].astype(BF16).reshape(N_EXPERTS // PEER_ET, PEER_ET, D_MODEL).transpose(0, 2, 1)
        final = l == DEPTH - 1

        zero_hist = (jnp.zeros(((CONV_A - 1) * bp, W_A), F32), jnp.zeros((POOL_BUF * bp, W_B), F32),
                     jnp.zeros(((CONV_C - 1) * bp, W_C), F32), jnp.zeros((bp, W_C), F32),
                     jnp.zeros(((CONV_D - 1) * bp, W_D), F32))
        u = _inproj(xp, g1, w_in_b, 1024, 1536)
        xp, *st = _mixer(u, xp, zero_hist, lp, PROMPT_NB, PROMPT_TT, True, 0)
        for acc, s in zip(outs_p, st):
            acc.append(s)
        xp = _peer(xp, g2, wqt, keys, u_tab, vt_tab, fg, final, PEER_TB, PEER_ET, PEER_TS)

        hist = (_to_chunks(state_conv_a[l], SAMPLE_NB), _to_chunks(state_pool[l], SAMPLE_NB),
                _to_chunks(state_conv_c[l], SAMPLE_NB), state_lru_h[l], _to_chunks(state_conv_d[l], SAMPLE_NB))
        u = _inproj(xs, g1, w_in_b, 512, 1536)
        xs, *st = _mixer(u, xs, hist, lp, SAMPLE_NB, ts, False, PAST_LEN)
        for acc, s in zip(outs_s, st):
            acc.append(s)
        xs = _peer(xs, g2, wqt, keys, u_tab, vt_tab, fg, final, PEER_TB, PEER_ET, PEER_TS)

    y_prompt = _from_chunks(xp, PROMPT_NB, bp, tp)
    y_sample = _from_chunks(xs, SAMPLE_NB, bs, ts)

    def states(outs, nb, b):
        res = []
        for idx, hist_len in ((0, CONV_A - 1), (1, POOL_BUF), (2, CONV_C - 1), (3, None), (4, CONV_D - 1)):
            if hist_len is None:
                res.append(jnp.stack(outs[idx]))
            else:
                res.append(jnp.stack([_from_chunks(s, nb, b, hist_len) for s in outs[idx]]))
        return res

    ca_p, pool_p, cc_p, h_p, cd_p = states(outs_p, PROMPT_NB, bp)
    ca_s, pool_s, cc_s, h_s, cd_s = states(outs_s, SAMPLE_NB, bs)
    return (y_prompt, y_sample, ca_p, ca_s, pool_p, pool_s, cc_p, cc_s, h_p, h_s, cd_p, cd_s)
```

```python
import functools

import jax
import jax.numpy as jnp
from jax import lax
from jax.experimental import pallas as pl
from jax.experimental.pallas import tpu as pltpu

F32 = jnp.float32
BF16 = jnp.bfloat16

D_MODEL = 1024
DEPTH = 2
PAST_LEN = 16384
W_A = 512
CONV_A = 3
W_B = 512
POOL_WINDOWS = (2, 4, 8, 16)
POOL_GROUP = 128
POOL_BUF = 15
W_C = 1024
LRU_HEADS = 8
LRU_HEAD_DIM = 128
CONV_C = 4
LRU_C = 8.0
W_D = 512
CONV_D = 31
OFF_B = 3 * W_A
OFF_C = OFF_B + W_B
OFF_D = OFF_C + 2 * W_C
OFF_G = OFF_D + 2 * W_D
IN_TOTAL = OFF_G + 4 * D_MODEL
PEER_HEADS = 8
N_KEYS = 128
N_EXPERTS = N_KEYS * N_KEYS
KEY_DIM = 128
PEER_TOPK = 16
EPS = 1e-6

LANES = 128
VMEM_LIMIT = 56 * 1024 * 1024

GELU_C0 = 0.7978845608028654
GELU_C1 = 0.044715
NEG_INF = float("-inf")


def _sigmoid(x):
    return 1.0 / (1.0 + jnp.exp(-x))


def _gelu(x):
    return 0.5 * x * (1.0 + jnp.tanh(GELU_C0 * (x + GELU_C1 * (x * x * x))))


def _gelu_sigmoid_form(x):
    q = x * (x * x * (-2.0 * GELU_C0 * GELU_C1) + (-2.0 * GELU_C0))
    return x / (1.0 + jnp.exp(q))


def _rms(x, g):
    ms = jnp.mean(x * x, axis=-1, keepdims=True)
    return x * lax.rsqrt(ms + EPS) * g


def _bdot(a, b):
    return jnp.dot(a.astype(BF16), b.astype(BF16), preferred_element_type=F32)


def _inproj_kernel(x_ref, g_ref, w_ref, u_ref, xn_ref):
    @pl.when(pl.program_id(1) == 0)
    def _():
        xn_ref[...] = _rms(x_ref[...], g_ref[...]).astype(BF16)

    u_ref[...] = jnp.dot(xn_ref[...], w_ref[...], preferred_element_type=F32)


def _inproj(x, g, w_bf16, tm, tn):
    n = x.shape[0]
    return pl.pallas_call(
        _inproj_kernel,
        grid=(n // tm, IN_TOTAL // tn),
        in_specs=[
            pl.BlockSpec((tm, D_MODEL), lambda i, j: (i, 0)),
            pl.BlockSpec((1, D_MODEL), lambda i, j: (0, 0)),
            pl.BlockSpec((D_MODEL, tn), lambda i, j: (0, j)),
        ],
        out_specs=pl.BlockSpec((tm, tn), lambda i, j: (i, j)),
        out_shape=jax.ShapeDtypeStruct((n, IN_TOTAL), F32),
        scratch_shapes=[pltpu.VMEM((tm, D_MODEL), BF16)],
        compiler_params=pltpu.CompilerParams(
            dimension_semantics=("arbitrary", "arbitrary"), vmem_limit_bytes=VMEM_LIMIT),
        name="inproj",
    )(x, g, w_bf16)


def _mixer_kernel(nb, tt, carry, offset,
                  u_ref, x_ref, ha_ref, hb_ref, hc_ref, h0_ref, hd_ref,
                  caw_ref, pw_ref, ps_ref, ccw_ref, ccb_ref, wa_ref, ba_ref, wi_ref, bi_ref, lam_ref,
                  cdw_ref, cdb_ref, lng_ref, lnb_ref, bg_ref,
                  wpa_ref, wpb_ref, wpc_ref, wpd_ref, wout_ref,
                  xo_ref, sa_ref, sb_ref, sc_ref, hl_ref, sd_ref,
                  ea_ref, eb_ref, ec_ref, ed_ref, h_ref, a_ref, bx_ref, hs_ref, yb_ref, gate_ref):
    r = tt * nb
    step = pl.program_id(0)
    first = (step == 0) if carry else None

    def load_hist():
        ea_ref[0:(CONV_A - 1) * nb, :] = ha_ref[...]
        eb_ref[0:POOL_BUF * nb, :] = hb_ref[...]
        ec_ref[0:(CONV_C - 1) * nb, :] = hc_ref[...]
        ed_ref[0:(CONV_D - 1) * nb, :] = hd_ref[...]
        h_ref[...] = h0_ref[...]

    if carry:
        pl.when(first)(load_hist)
    else:
        load_hist()

    z = u_ref[:, W_A:2 * W_A] * u_ref[:, 2 * W_A:3 * W_A]
    ea_ref[(CONV_A - 1) * nb:, :] = z
    y = caw_ref[0:1, :] * ea_ref[0:r, :]
    for k in range(1, CONV_A):
        y = y + caw_ref[k:k + 1, :] * ea_ref[k * nb:k * nb + r, :]
    ya = u_ref[:, 0:W_A] * y

    eb_ref[POOL_BUF * nb:, :] = u_ref[:, OFF_B:OFF_C]
    row = lax.broadcasted_iota(jnp.int32, (r, POOL_GROUP), 0)
    tpos = (row >> (nb.bit_length() - 1)) + (step * tt if carry else 0) + offset
    for g, w in enumerate(POOL_WINDOWS):
        sl = slice(g * POOL_GROUP, (g + 1) * POOL_GROUP)
        acc = eb_ref[POOL_BUF * nb:POOL_BUF * nb + r, sl]
        for k in range(1, w):
            acc = acc + eb_ref[(POOL_BUF - k) * nb:(POOL_BUF - k) * nb + r, sl]
        cnt = jnp.minimum(tpos + 1, w).astype(F32)
        p = acc / cnt - eb_ref[POOL_BUF * nb:POOL_BUF * nb + r, sl]
        yb_ref[:, sl] = _bdot(p, pw_ref[g]) * ps_ref[:, sl]

    ec_ref[(CONV_C - 1) * nb:, :] = u_ref[:, OFF_C:OFF_C + W_C]
    xconv = ccw_ref[0:1, :] * ec_ref[0:r, :]
    for k in range(1, CONV_C):
        xconv = xconv + ccw_ref[k:k + 1, :] * ec_ref[k * nb:k * nb + r, :]
    xconv = xconv + ccb_ref[...]
    lam = lam_ref[...]
    nsp = -LRU_C * (jnp.maximum(-lam, 0.0) + jnp.log(1.0 + jnp.exp(-jnp.abs(lam))))
    for hh in range(LRU_HEADS):
        sl = slice(hh * LRU_HEAD_DIM, (hh + 1) * LRU_HEAD_DIM)
        xh = xconv[:, sl]
        xhb = xh.astype(BF16)
        rg = _sigmoid(jnp.dot(xhb, wa_ref[hh], preferred_element_type=F32) + ba_ref[:, sl])
        ig = _sigmoid(jnp.dot(xhb, wi_ref[hh], preferred_element_type=F32) + bi_ref[:, sl])
        log_a = rg * nsp[:, sl]
        a = jnp.exp(log_a)
        a_ref[:, sl] = a
        bx_ref[:, sl] = jnp.sqrt(1.0 - a * a) * (ig * xh)

    if tt <= 8:
        for t in range(tt):
            h = a_ref[t * nb:(t + 1) * nb, :] * h_ref[...] + bx_ref[t * nb:(t + 1) * nb, :]
            h_ref[...] = h
            hs_ref[t * nb:(t + 1) * nb, :] = h
    else:
        def scan_body(t, h):
            rows = pl.ds(pl.multiple_of(t * nb, nb), nb)
            h = a_ref[rows, :] * h + bx_ref[rows, :]
            hs_ref[rows, :] = h
            return h
        h_ref[...] = lax.fori_loop(0, tt, scan_body, h_ref[...])
    yc = hs_ref[...] * _gelu(u_ref[:, OFF_C + W_C:OFF_D])

    ed_ref[(CONV_D - 1) * nb:, :] = u_ref[:, OFF_D:OFF_D + W_D] * _sigmoid(u_ref[:, OFF_D + W_D:OFF_G])
    c = cdw_ref[0:1, :] * ed_ref[0:r, :]
    for k in range(1, CONV_D):
        c = c + cdw_ref[k:k + 1, :] * ed_ref[k * nb:k * nb + r, :]
    c = c + cdb_ref[...]
    mu = jnp.mean(c, axis=-1, keepdims=True)
    cc = c - mu
    var = jnp.mean(cc * cc, axis=-1, keepdims=True)
    ln = cc * lax.rsqrt(var + EPS) * lng_ref[...] + lnb_ref[...]
    yd = ln * _sigmoid(ln)

    gate_ref[...] = _sigmoid(u_ref[:, OFF_G:] + bg_ref[...])
    m = gate_ref[:, 0:D_MODEL] * _bdot(ya, wpa_ref[...])
    m = m + gate_ref[:, D_MODEL:2 * D_MODEL] * _bdot(yb_ref[...], wpb_ref[...])
    m = m + gate_ref[:, 2 * D_MODEL:3 * D_MODEL] * _bdot(yc, wpc_ref[...])
    m = m + gate_ref[:, 3 * D_MODEL:] * _bdot(yd, wpd_ref[...])
    xo_ref[...] = x_ref[...] + _bdot(m, wout_ref[...])

    def store_state():
        sa_ref[...] = ea_ref[r:r + (CONV_A - 1) * nb, :]
        sb_ref[...] = eb_ref[r:r + POOL_BUF * nb, :]
        sc_ref[...] = ec_ref[r:r + (CONV_C - 1) * nb, :]
        sd_ref[...] = ed_ref[r:r + (CONV_D - 1) * nb, :]
        hl_ref[...] = h_ref[...]

    if carry:
        pl.when(step == pl.num_programs(0) - 1)(store_state)
        for e_ref, hist in ((ea_ref, CONV_A - 1), (eb_ref, POOL_BUF), (ec_ref, CONV_C - 1), (ed_ref, CONV_D - 1)):
            total = hist * nb
            for s in range(0, total, r):
                n = min(r, total - s)
                e_ref[s:s + n, :] = e_ref[r + s:r + s + n, :]
    else:
        store_state()


def _mixer(u, x, hists, lp, nb, tt, carry, offset):
    n = x.shape[0]
    r = nb * tt
    steps = n // r
    ha, hb, hc, h0, hd = hists
    widths = (W_A, W_B, W_C, W_D)
    hist_rows = ((CONV_A - 1) * nb, POOL_BUF * nb, (CONV_C - 1) * nb, (CONV_D - 1) * nb)

    def chunk_map(i):
        return (0, 0) if carry else (i, 0)

    def const2(i):
        return (0, 0)

    def const3(i):
        return (0, 0, 0)

    def full(a):
        return pl.BlockSpec(a.shape, const2 if a.ndim == 2 else const3)

    params = [lp['conv_a_w'], lp['pool_w'], lp['pool_scale'], lp['conv_c_w'], lp['conv_c_b'],
              lp['lru_wa'], lp['lru_ba'], lp['lru_wi'], lp['lru_bi'], lp['lru_lambda'],
              lp['conv_d_w'], lp['conv_d_b'], lp['ln_d_g'], lp['ln_d_b'], lp['b_gate'],
              lp['w_pa'], lp['w_pb'], lp['w_pc'], lp['w_pd'], lp['w_out']]
    in_specs = [
        pl.BlockSpec((r, IN_TOTAL), lambda i: (i, 0)),
        pl.BlockSpec((r, D_MODEL), lambda i: (i, 0)),
        pl.BlockSpec((hist_rows[0], W_A), chunk_map),
        pl.BlockSpec((hist_rows[1], W_B), chunk_map),
        pl.BlockSpec((hist_rows[2], W_C), chunk_map),
        pl.BlockSpec((nb, W_C), chunk_map),
        pl.BlockSpec((hist_rows[3], W_D), chunk_map),
    ] + [full(p) for p in params]
    n_state = 1 if carry else steps
    out_shape = (
        jax.ShapeDtypeStruct((n, D_MODEL), F32),
        jax.ShapeDtypeStruct((n_state * hist_rows[0], W_A), F32),
        jax.ShapeDtypeStruct((n_state * hist_rows[1], W_B), F32),
        jax.ShapeDtypeStruct((n_state * hist_rows[2], W_C), F32),
        jax.ShapeDtypeStruct((n_state * nb, W_C), F32),
        jax.ShapeDtypeStruct((n_state * hist_rows[3], W_D), F32),
    )
    out_specs = (
        pl.BlockSpec((r, D_MODEL), lambda i: (i, 0)),
        pl.BlockSpec((hist_rows[0], W_A), chunk_map),
        pl.BlockSpec((hist_rows[1], W_B), chunk_map),
        pl.BlockSpec((hist_rows[2], W_C), chunk_map),
        pl.BlockSpec((nb, W_C), chunk_map),
        pl.BlockSpec((hist_rows[3], W_D), chunk_map),
    )
    scratch = [pltpu.VMEM((hr + r, w), F32) for hr, w in zip(hist_rows, widths)]
    scratch += [
        pltpu.VMEM((nb, W_C), F32),
        pltpu.VMEM((r, W_C), F32),
        pltpu.VMEM((r, W_C), F32),
        pltpu.VMEM((r, W_C), F32),
        pltpu.VMEM((r, W_B), F32),
        pltpu.VMEM((r, 4 * D_MODEL), F32),
    ]
    return pl.pallas_call(
        functools.partial(_mixer_kernel, nb, tt, carry, offset),
        grid=(steps,),
        in_specs=in_specs,
        out_specs=out_specs,
        out_shape=out_shape,
        scratch_shapes=scratch,
        compiler_params=pltpu.CompilerParams(
            dimension_semantics=("arbitrary",), vmem_limit_bytes=VMEM_LIMIT),
        name="mixer_carry" if carry else "mixer_batch",
    )(u, x, ha, hb, hc, h0, hd, *params)


N_TOP = PEER_TOPK + 1
SV_ROWS = 24


PRO_LANES = 2 * LANES
SUBLANES = 8
MXU_COLS = 256


def _batcher_pairs(n):
    pairs = []

    def merge(lo, hi, r):
        step = r * 2
        if step < hi - lo:
            merge(lo, hi, step)
            merge(lo + r, hi, step)
            pairs.extend((i, i + r) for i in range(lo + r, hi - r, step))
        else:
            pairs.append((lo, lo + r))

    def sort(lo, hi):
        if hi - lo >= 1:
            mid = lo + (hi - lo) // 2
            sort(lo, mid)
            sort(mid + 1, hi)
            merge(lo, hi, 1)

    sort(0, n - 1)
    return pairs


def _sort_desc(ws):
    n = 1
    while n < len(ws):
        n *= 2
    ws = list(ws) + [None] * (n - len(ws))
    for i, j in _batcher_pairs(n):
        a, b = ws[i], ws[j]
        if b is None:
            continue
        if a is None:
            ws[i], ws[j] = b, None
        else:
            ws[i], ws[j] = jnp.maximum(a, b), jnp.minimum(a, b)
    return [w for w in ws if w is not None]


def _top_rows(ws, n):
    out = []
    for k in range(n):
        m = jnp.max(ws[0], axis=0, keepdims=True)
        out.append(m)
        remaining = n - 1 - k
        if remaining == 0:
            break
        hit = ws[0] >= m
        ws = [jnp.where(hit, ws[i + 1] if i + 1 < len(ws) else NEG_INF, ws[i])
              for i in range(min(len(ws), remaining))]
    return out


def _peer_kernel(tb, et, ts, final,
                 x_ref, g_ref, wqt_ref, keys_ref, u_ref, vt_ref, vtp_ref, fg_ref,
                 o_ref,
                 xnt_ref, qt_ref, s_ref, sv_ref, a_ref, c_ref, r2_ref, e2_ref, ht_ref, ct_ref, acc_ref):
    npl = tb // PRO_LANES
    rows_per_tile = et // N_KEYS
    j = pl.program_id(1)

    @pl.when(j == 0)
    def _():
        xn = _rms(x_ref[...], g_ref[...])
        xnt_ref[...] = xn.T.astype(BF16)
        qt_ref[...] = jnp.dot(wqt_ref[...], xnt_ref[...], preferred_element_type=F32).astype(BF16)
        for hc in range(2 * PEER_HEADS):
            s = jnp.dot(keys_ref[hc], qt_ref[hc * KEY_DIM:(hc + 1) * KEY_DIM, :],
                        preferred_element_type=F32)
            for pt in range(npl):
                s_ref[hc, pt] = s[:, pt * PRO_LANES:(pt + 1) * PRO_LANES]

        def top_body(idx, carry):
            hc = idx // npl
            pt = idx % npl
            ws = _sort_desc([s_ref[hc, pt, i * SUBLANES:(i + 1) * SUBLANES, :] for i in range(N_KEYS // SUBLANES)])
            rows = _top_rows(ws, N_TOP)
            sv_ref[hc, pt] = jnp.full((SV_ROWS, PRO_LANES), NEG_INF, F32)
            for k in range(N_TOP):
                sv_ref[hc, pt, k:k + 1, :] = rows[k]
            return carry

        lax.fori_loop(0, 2 * PEER_HEADS * npl, top_body, 0)

        def thr_body(idx, carry):
            h = idx // npl
            pt = idx % npl
            sv1 = sv_ref[2 * h, pt]
            sv2 = sv_ref[2 * h + 1, pt]
            pieces = [sv1[0:1, :] + sv2[i * SUBLANES:(i + 1) * SUBLANES, :] for i in range(SV_ROWS // SUBLANES)]
            pieces += [sv1[a:a + 1, :] + sv2[0:SUBLANES, :] for a in range(1, SUBLANES)]
            pieces += [sv1[i * SUBLANES:(i + 1) * SUBLANES, :] + sv2[0:1, :] for i in range(1, SV_ROWS // SUBLANES)]
            best = _top_rows(_sort_desc(pieces), N_TOP)
            zsum = jnp.ones_like(best[0])
            for k in range(1, PEER_TOPK):
                zsum = zsum + jnp.exp(best[k] - best[0])
            thr = 0.5 * (best[PEER_TOPK - 1] + best[PEER_TOPK])
            s1 = s_ref[2 * h, pt]
            s2 = s_ref[2 * h + 1, pt]
            a_ref[h, pt] = jnp.exp(s1 - sv1[0:1, :]) / zsum
            e2_ref[h, pt] = jnp.exp(s2 - sv2[0:1, :]).astype(BF16)
            bthr = thr - s1
            rank2 = jnp.full(s2.shape, float(PEER_TOPK), F32)
            count = jnp.full(s1.shape, float(PEER_TOPK), F32)
            for k in range(PEER_TOPK - 1, -1, -1):
                rank2 = jnp.where(sv2[k:k + 1, :] <= s2, float(k), rank2)
                count = jnp.where(sv2[k:k + 1, :] < bthr, float(k), count)
            r2_ref[h, pt] = rank2.astype(BF16)
            c_ref[h, pt] = count
            return carry

        lax.fori_loop(0, PEER_HEADS * npl, thr_body, 0)
        acc_ref[...] = jnp.zeros_like(acc_ref)
        ct_ref[(ts - 1) * et:ts * et, tb - MXU_COLS:tb] = jnp.zeros((et, MXU_COLS), BF16)

    pieces = [(t, c) for t in range(ts) for c in range(tb // MXU_COLS)]
    units = [(t, c, rr) for (t, c) in pieces for rr in range(rows_per_tile)]
    out_rows = D_MODEL // rows_per_tile

    def pre_activations(t, c, rr):
        er = slice(t * et + rr * N_KEYS, t * et + (rr + 1) * N_KEYS)
        cols = slice(c * MXU_COLS, (c + 1) * MXU_COLS)
        ht_ref[er, cols] = jnp.dot(u_ref[er, :], xnt_ref[:, cols], preferred_element_type=F32).astype(BF16)

    def output_term(t, c, m, v_ref=None):
        rows = slice(t * et, (t + 1) * et)
        cols = slice(c * MXU_COLS, (c + 1) * MXU_COLS)
        mr = slice(m * out_rows, (m + 1) * out_rows)
        v = vt_ref[t, mr, :] if v_ref is None else v_ref[0, mr, :]
        acc_ref[mr, cols] += jnp.dot(v, ct_ref[rows, cols], preferred_element_type=F32)

    def coefficient_tile(t, c, rr, l):
        i1 = (j * ts + t) * rows_per_tile + rr
        lt = c * (MXU_COLS // LANES) + l
        pt = lt // 2
        ln = slice((lt % 2) * LANES, (lt % 2 + 1) * LANES)
        g = jnp.zeros((N_KEYS, LANES), BF16)
        zero = jnp.zeros((N_KEYS, LANES), BF16)
        for h in range(PEER_HEADS):
            a_row = jnp.broadcast_to(a_ref[h, pt, pl.ds(i1, 1), :][:, ln], (N_KEYS, LANES)).astype(BF16)
            c_row = jnp.broadcast_to(c_ref[h, pt, pl.ds(i1, 1), :][:, ln], (N_KEYS, LANES)).astype(BF16)
            g = g + jnp.where(r2_ref[h, pt, :, ln] < c_row, e2_ref[h, pt, :, ln], zero) * a_row
        er = slice(t * et + rr * N_KEYS, t * et + (rr + 1) * N_KEYS)
        tl = slice(lt * LANES, (lt + 1) * LANES)
        ct_ref[er, tl] = g * _gelu_sigmoid_form(ht_ref[er, tl])

    pre_activations(*units[0])
    for n, (t, c, rr) in enumerate(units):
        if n + 1 < len(units):
            pre_activations(*units[n + 1])
        coefficient_tile(t, c, rr, 0)
        if n >= rows_per_tile:
            pt_, pc_, _ = units[n - rows_per_tile]
            output_term(pt_, pc_, rr)
        else:
            output_term(*pieces[-1], rr, v_ref=vtp_ref)
        coefficient_tile(t, c, rr, 1)

    @pl.when(j == pl.num_programs(1) - 1)
    def _():
        for m in range(rows_per_tile):
            output_term(*pieces[-1], m)
        y = x_ref[...] + acc_ref[...].T
        if final:
            y = _rms(y, fg_ref[...])
        o_ref[...] = y


def _peer(x, g, wqt, keys, u_tab, vt_tiles, final_g, final, tb, et, ts):
    n = x.shape[0]
    npl = tb // PRO_LANES
    return pl.pallas_call(
        functools.partial(_peer_kernel, tb, et, ts, final),
        grid=(n // tb, N_EXPERTS // (ts * et)),
        in_specs=[
            pl.BlockSpec((tb, D_MODEL), lambda i, j: (i, 0)),
            pl.BlockSpec((1, D_MODEL), lambda i, j: (0, 0)),
            pl.BlockSpec(wqt.shape, lambda i, j: (0, 0)),
            pl.BlockSpec(keys.shape, lambda i, j: (0, 0, 0)),
            pl.BlockSpec((ts * et, D_MODEL), lambda i, j: (j, 0)),
            pl.BlockSpec((ts, D_MODEL, et), lambda i, j: (j, 0, 0)),
            pl.BlockSpec((1, D_MODEL, et), lambda i, j: (jnp.maximum(j * ts - 1, 0), 0, 0)),
            pl.BlockSpec((1, D_MODEL), lambda i, j: (0, 0)),
        ],
        out_specs=pl.BlockSpec((tb, D_MODEL), lambda i, j: (i, 0)),
        out_shape=jax.ShapeDtypeStruct((n, D_MODEL), F32),
        scratch_shapes=[
            pltpu.VMEM((D_MODEL, tb), BF16),
            pltpu.VMEM((2 * PEER_HEADS * KEY_DIM, tb), BF16),
            pltpu.VMEM((2 * PEER_HEADS, npl, N_KEYS, PRO_LANES), F32),
            pltpu.VMEM((2 * PEER_HEADS, npl, SV_ROWS, PRO_LANES), F32),
            pltpu.VMEM((PEER_HEADS, npl, N_KEYS, PRO_LANES), F32),
            pltpu.VMEM((PEER_HEADS, npl, N_KEYS, PRO_LANES), F32),
            pltpu.VMEM((PEER_HEADS, npl, N_KEYS, PRO_LANES), BF16),
            pltpu.VMEM((PEER_HEADS, npl, N_KEYS, PRO_LANES), BF16),
            pltpu.VMEM((ts * et, tb), BF16),
            pltpu.VMEM((ts * et, tb), BF16),
            pltpu.VMEM((D_MODEL, tb), F32),
        ],
        compiler_params=pltpu.CompilerParams(
            dimension_semantics=("arbitrary", "arbitrary"), vmem_limit_bytes=VMEM_LIMIT),
        name="peer_final" if final else "peer",
    )(x, g, wqt, keys, u_tab, vt_tiles, vt_tiles, final_g)


PROMPT_NB = 8
PROMPT_TT = 32
SAMPLE_NB = 32
PEER_TB = 512
PEER_ET = 512
PEER_TS = 4


def _to_chunks(a, nb):
    b, t, c = a.shape
    return a.reshape(b // nb, nb, t, c).transpose(0, 2, 1, 3).reshape(b * t, c)


def _from_chunks(a, nb, b, t):
    c = a.shape[-1]
    return a.reshape(b // nb, t, nb, c).transpose(0, 2, 1, 3).reshape(b, t, c)


def kernel(x_prompt, x_sample, state_conv_a, state_pool, state_conv_c, state_lru_h, state_conv_d, norm1_g, w_in, b_gate, conv_a_w, pool_w, pool_scale, conv_c_w, conv_c_b, lru_wa, lru_ba, lru_wi, lru_bi, lru_lambda, conv_d_w, conv_d_b, ln_d_g, ln_d_b, w_pa, w_pb, w_pc, w_pd, w_out, norm2_g, peer_wq, peer_keys, peer_u, peer_v, final_g):
    bp, tp, _ = x_prompt.shape
    bs, ts, _ = x_sample.shape
    xp = _to_chunks(x_prompt, PROMPT_NB)
    xs = _to_chunks(x_sample, SAMPLE_NB)
    fg = final_g.reshape(1, D_MODEL)

    outs_p = [[] for _ in range(5)]
    outs_s = [[] for _ in range(5)]
    for l in range(DEPTH):
        lp = {
            'conv_a_w': conv_a_w[l], 'pool_w': pool_w[l].astype(BF16), 'pool_scale': pool_scale[l].reshape(1, W_B),
            'conv_c_w': conv_c_w[l], 'conv_c_b': conv_c_b[l].reshape(1, W_C),
            'lru_wa': lru_wa[l].astype(BF16), 'lru_ba': lru_ba[l].reshape(1, W_C),
            'lru_wi': lru_wi[l].astype(BF16), 'lru_bi': lru_bi[l].reshape(1, W_C),
            'lru_lambda': lru_lambda[l].reshape(1, W_C),
            'conv_d_w': conv_d_w[l], 'conv_d_b': conv_d_b[l].reshape(1, W_D),
            'ln_d_g': ln_d_g[l].reshape(1, W_D), 'ln_d_b': ln_d_b[l].reshape(1, W_D),
            'b_gate': b_gate[l].reshape(1, 4 * D_MODEL),
            'w_pa': w_pa[l].astype(BF16), 'w_pb': w_pb[l].astype(BF16), 'w_pc': w_pc[l].astype(BF16),
            'w_pd': w_pd[l].astype(BF16), 'w_out': w_out[l].astype(BF16),
        }
        g1 = norm1_g[l].reshape(1, D_MODEL)
        g2 = norm2_g[l].reshape(1, D_MODEL)
        w_in_b = w_in[l].astype(BF16)
        wqt = peer_wq[l].T.astype(BF16)
        keys = peer_keys[l].reshape(2 * PEER_HEADS, N_KEYS, KEY_DIM).astype(BF16)
        u_tab = peer_u[l].astype(BF16)
        vt_tab = peer_v[l].astype(BF16).reshape(N_EXPERTS // PEER_ET, PEER_ET, D_MODEL).transpose(0, 2, 1)
        final = l == DEPTH - 1

        zero_hist = (jnp.zeros(((CONV_A - 1) * bp, W_A), F32), jnp.zeros((POOL_BUF * bp, W_B), F32),
                     jnp.zeros(((CONV_C - 1) * bp, W_C), F32), jnp.zeros((bp, W_C), F32),
                     jnp.zeros(((CONV_D - 1) * bp, W_D), F32))
        u = _inproj(xp, g1, w_in_b, 1024, 1536)
        xp, *st = _mixer(u, xp, zero_hist, lp, PROMPT_NB, PROMPT_TT, True, 0)
        for acc, s in zip(outs_p, st):
            acc.append(s)
        xp = _peer(xp, g2, wqt, keys, u_tab, vt_tab, fg, final, PEER_TB, PEER_ET, PEER_TS)

        hist = (_to_chunks(state_conv_a[l], SAMPLE_NB), _to_chunks(state_pool[l], SAMPLE_NB),
                _to_chunks(state_conv_c[l], SAMPLE_NB), state_lru_h[l], _to_chunks(state_conv_d[l], SAMPLE_NB))
        u = _inproj(xs, g1, w_in_b, 512, 1536)
        xs, *st = _mixer(u, xs, hist, lp, SAMPLE_NB, ts, False, PAST_LEN)
        for acc, s in zip(outs_s, st):
            acc.append(s)
        xs = _peer(xs, g2, wqt, keys, u_tab, vt_tab, fg, final, PEER_TB, PEER_ET, PEER_TS)

    y_prompt = _from_chunks(xp, PROMPT_NB, bp, tp)
    y_sample = _from_chunks(xs, SAMPLE_NB, bs, ts)

    def states(outs, nb, b):
        res = []
        for idx, hist_len in ((0, CONV_A - 1), (1, POOL_BUF), (2, CONV_C - 1), (3, None), (4, CONV_D - 1)):
            if hist_len is None:
                res.append(jnp.stack(outs[idx]))
            else:
                res.append(jnp.stack([_from_chunks(s, nb, b, hist_len) for s in outs[idx]]))
        return res

    ca_p, pool_p, cc_p, h_p, cd_p = states(outs_p, PROMPT_NB, bp)
    ca_s, pool_s, cc_s, h_s, cd_s = states(outs_s, SAMPLE_NB, bs)
    return (y_prompt, y_sample, ca_p, ca_s, pool_p, pool_s, cc_p, cc_s, h_p, h_s, cd_p, cd_s)
```

```python
import functools

import jax
import jax.numpy as jnp
from jax import lax
from jax.experimental import pallas as pl
from jax.experimental.pallas import tpu as pltpu

F32 = jnp.float32
BF16 = jnp.bfloat16

D_MODEL = 1024
DEPTH = 2
PAST_LEN = 16384
W_A = 512
CONV_A = 3
W_B = 512
POOL_WINDOWS = (2, 4, 8, 16)
POOL_GROUP = 128
POOL_BUF = 15
W_C = 1024
LRU_HEADS = 8
LRU_HEAD_DIM = 128
CONV_C = 4
LRU_C = 8.0
W_D = 512
CONV_D = 31
OFF_B = 3 * W_A
OFF_C = OFF_B + W_B
OFF_D = OFF_C + 2 * W_C
OFF_G = OFF_D + 2 * W_D
IN_TOTAL = OFF_G + 4 * D_MODEL
PEER_HEADS = 8
N_KEYS = 128
N_EXPERTS = N_KEYS * N_KEYS
KEY_DIM = 128
PEER_TOPK = 16
EPS = 1e-6

LANES = 128
VMEM_LIMIT = 56 * 1024 * 1024

GELU_C0 = 0.7978845608028654
GELU_C1 = 0.044715
NEG_INF = float("-inf")


def _sigmoid(x):
    return 1.0 / (1.0 + jnp.exp(-x))


def _gelu(x):
    return 0.5 * x * (1.0 + jnp.tanh(GELU_C0 * (x + GELU_C1 * (x * x * x))))


def _gelu_sigmoid_form(x):
    q = x * (x * x * (-2.0 * GELU_C0 * GELU_C1) + (-2.0 * GELU_C0))
    return x / (1.0 + jnp.exp(q))


def _rms(x, g):
    ms = jnp.mean(x * x, axis=-1, keepdims=True)
    return x * lax.rsqrt(ms + EPS) * g


def _bdot(a, b):
    return jnp.dot(a.astype(BF16), b.astype(BF16), preferred_element_type=F32)


def _inproj_kernel(x_ref, g_ref, w_ref, u_ref, xn_ref):
    @pl.when(pl.program_id(1) == 0)
    def _():
        xn_ref[...] = _rms(x_ref[...], g_ref[...]).astype(BF16)

    u_ref[...] = jnp.dot(xn_ref[...], w_ref[...], preferred_element_type=F32)


def _inproj(x, g, w_bf16, tm, tn):
    n = x.shape[0]
    return pl.pallas_call(
        _inproj_kernel,
        grid=(n // tm, IN_TOTAL // tn),
        in_specs=[
            pl.BlockSpec((tm, D_MODEL), lambda i, j: (i, 0)),
            pl.BlockSpec((1, D_MODEL), lambda i, j: (0, 0)),
            pl.BlockSpec((D_MODEL, tn), lambda i, j: (0, j)),
        ],
        out_specs=pl.BlockSpec((tm, tn), lambda i, j: (i, j)),
        out_shape=jax.ShapeDtypeStruct((n, IN_TOTAL), F32),
        scratch_shapes=[pltpu.VMEM((tm, D_MODEL), BF16)],
        compiler_params=pltpu.CompilerParams(
            dimension_semantics=("arbitrary", "arbitrary"), vmem_limit_bytes=VMEM_LIMIT),
        name="inproj",
    )(x, g, w_bf16)


def _mixer_kernel(nb, tt, carry, offset,
                  u_ref, x_ref, ha_ref, hb_ref, hc_ref, h0_ref, hd_ref,
                  caw_ref, pw_ref, ps_ref, ccw_ref, ccb_ref, wa_ref, ba_ref, wi_ref, bi_ref, lam_ref,
                  cdw_ref, cdb_ref, lng_ref, lnb_ref, bg_ref,
                  wpa_ref, wpb_ref, wpc_ref, wpd_ref, wout_ref,
                  xo_ref, sa_ref, sb_ref, sc_ref, hl_ref, sd_ref,
                  ea_ref, eb_ref, ec_ref, ed_ref, h_ref, a_ref, bx_ref, hs_ref, yb_ref, gate_ref):
    r = tt * nb
    step = pl.program_id(0)
    first = (step == 0) if carry else None

    def load_hist():
        ea_ref[0:(CONV_A - 1) * nb, :] = ha_ref[...]
        eb_ref[0:POOL_BUF * nb, :] = hb_ref[...]
        ec_ref[0:(CONV_C - 1) * nb, :] = hc_ref[...]
        ed_ref[0:(CONV_D - 1) * nb, :] = hd_ref[...]
        h_ref[...] = h0_ref[...]

    if carry:
        pl.when(first)(load_hist)
    else:
        load_hist()

    z = u_ref[:, W_A:2 * W_A] * u_ref[:, 2 * W_A:3 * W_A]
    ea_ref[(CONV_A - 1) * nb:, :] = z
    y = caw_ref[0:1, :] * ea_ref[0:r, :]
    for k in range(1, CONV_A):
        y = y + caw_ref[k:k + 1, :] * ea_ref[k * nb:k * nb + r, :]
    ya = u_ref[:, 0:W_A] * y

    eb_ref[POOL_BUF * nb:, :] = u_ref[:, OFF_B:OFF_C]
    row = lax.broadcasted_iota(jnp.int32, (r, POOL_GROUP), 0)
    tpos = (row >> (nb.bit_length() - 1)) + (step * tt if carry else 0) + offset
    for g, w in enumerate(POOL_WINDOWS):
        sl = slice(g * POOL_GROUP, (g + 1) * POOL_GROUP)
        acc = eb_ref[POOL_BUF * nb:POOL_BUF * nb + r, sl]
        for k in range(1, w):
            acc = acc + eb_ref[(POOL_BUF - k) * nb:(POOL_BUF - k) * nb + r, sl]
        cnt = jnp.minimum(tpos + 1, w).astype(F32)
        p = acc / cnt - eb_ref[POOL_BUF * nb:POOL_BUF * nb + r, sl]
        yb_ref[:, sl] = _bdot(p, pw_ref[g]) * ps_ref[:, sl]

    ec_ref[(CONV_C - 1) * nb:, :] = u_ref[:, OFF_C:OFF_C + W_C]
    xconv = ccw_ref[0:1, :] * ec_ref[0:r, :]
    for k in range(1, CONV_C):
        xconv = xconv + ccw_ref[k:k + 1, :] * ec_ref[k * nb:k * nb + r, :]
    xconv = xconv + ccb_ref[...]
    lam = lam_ref[...]
    nsp = -LRU_C * (jnp.maximum(-lam, 0.0) + jnp.log(1.0 + jnp.exp(-jnp.abs(lam))))
    for hh in range(LRU_HEADS):
        sl = slice(hh * LRU_HEAD_DIM, (hh + 1) * LRU_HEAD_DIM)
        xh = xconv[:, sl]
        xhb = xh.astype(BF16)
        rg = _sigmoid(jnp.dot(xhb, wa_ref[hh], preferred_element_type=F32) + ba_ref[:, sl])
        ig = _sigmoid(jnp.dot(xhb, wi_ref[hh], preferred_element_type=F32) + bi_ref[:, sl])
        log_a = rg * nsp[:, sl]
        a = jnp.exp(log_a)
        a_ref[:, sl] = a
        bx_ref[:, sl] = jnp.sqrt(1.0 - a * a) * (ig * xh)

    if tt <= 8:
        for t in range(tt):
            h = a_ref[t * nb:(t + 1) * nb, :] * h_ref[...] + bx_ref[t * nb:(t + 1) * nb, :]
            h_ref[...] = h
            hs_ref[t * nb:(t + 1) * nb, :] = h
    else:
        def scan_body(t, h):
            rows = pl.ds(pl.multiple_of(t * nb, nb), nb)
            h = a_ref[rows, :] * h + bx_ref[rows, :]
            hs_ref[rows, :] = h
            return h
        h_ref[...] = lax.fori_loop(0, tt, scan_body, h_ref[...])
    yc = hs_ref[...] * _gelu(u_ref[:, OFF_C + W_C:OFF_D])

    ed_ref[(CONV_D - 1) * nb:, :] = u_ref[:, OFF_D:OFF_D + W_D] * _sigmoid(u_ref[:, OFF_D + W_D:OFF_G])
    c = cdw_ref[0:1, :] * ed_ref[0:r, :]
    for k in range(1, CONV_D):
        c = c + cdw_ref[k:k + 1, :] * ed_ref[k * nb:k * nb + r, :]
    c = c + cdb_ref[...]
    mu = jnp.mean(c, axis=-1, keepdims=True)
    cc = c - mu
    var = jnp.mean(cc * cc, axis=-1, keepdims=True)
    ln = cc * lax.rsqrt(var + EPS) * lng_ref[...] + lnb_ref[...]
    yd = ln * _sigmoid(ln)

    gate_ref[...] = _sigmoid(u_ref[:, OFF_G:] + bg_ref[...])
    m = gate_ref[:, 0:D_MODEL] * _bdot(ya, wpa_ref[...])
    m = m + gate_ref[:, D_MODEL:2 * D_MODEL] * _bdot(yb_ref[...], wpb_ref[...])
    m = m + gate_ref[:, 2 * D_MODEL:3 * D_MODEL] * _bdot(yc, wpc_ref[...])
    m = m + gate_ref[:, 3 * D_MODEL:] * _bdot(yd, wpd_ref[...])
    xo_ref[...] = x_ref[...] + _bdot(m, wout_ref[...])

    def store_state():
        sa_ref[...] = ea_ref[r:r + (CONV_A - 1) * nb, :]
        sb_ref[...] = eb_ref[r:r + POOL_BUF * nb, :]
        sc_ref[...] = ec_ref[r:r + (CONV_C - 1) * nb, :]
        sd_ref[...] = ed_ref[r:r + (CONV_D - 1) * nb, :]
        hl_ref[...] = h_ref[...]

    if carry:
        pl.when(step == pl.num_programs(0) - 1)(store_state)
        for e_ref, hist in ((ea_ref, CONV_A - 1), (eb_ref, POOL_BUF), (ec_ref, CONV_C - 1), (ed_ref, CONV_D - 1)):
            total = hist * nb
            for s in range(0, total, r):
                n = min(r, total - s)
                e_ref[s:s + n, :] = e_ref[r + s:r + s + n, :]
    else:
        store_state()


def _mixer(u, x, hists, lp, nb, tt, carry, offset):
    n = x.shape[0]
    r = nb * tt
    steps = n // r
    ha, hb, hc, h0, hd = hists
    widths = (W_A, W_B, W_C, W_D)
    hist_rows = ((CONV_A - 1) * nb, POOL_BUF * nb, (CONV_C - 1) * nb, (CONV_D - 1) * nb)

    def chunk_map(i):
        return (0, 0) if carry else (i, 0)

    def const2(i):
        return (0, 0)

    def const3(i):
        return (0, 0, 0)

    def full(a):
        return pl.BlockSpec(a.shape, const2 if a.ndim == 2 else const3)

    params = [lp['conv_a_w'], lp['pool_w'], lp['pool_scale'], lp['conv_c_w'], lp['conv_c_b'],
              lp['lru_wa'], lp['lru_ba'], lp['lru_wi'], lp['lru_bi'], lp['lru_lambda'],
              lp['conv_d_w'], lp['conv_d_b'], lp['ln_d_g'], lp['ln_d_b'], lp['b_gate'],
              lp['w_pa'], lp['w_pb'], lp['w_pc'], lp['w_pd'], lp['w_out']]
    in_specs = [
        pl.BlockSpec((r, IN_TOTAL), lambda i: (i, 0)),
        pl.BlockSpec((r, D_MODEL), lambda i: (i, 0)),
        pl.BlockSpec((hist_rows[0], W_A), chunk_map),
        pl.BlockSpec((hist_rows[1], W_B), chunk_map),
        pl.BlockSpec((hist_rows[2], W_C), chunk_map),
        pl.BlockSpec((nb, W_C), chunk_map),
        pl.BlockSpec((hist_rows[3], W_D), chunk_map),
    ] + [full(p) for p in params]
    n_state = 1 if carry else steps
    out_shape = (
        jax.ShapeDtypeStruct((n, D_MODEL), F32),
        jax.ShapeDtypeStruct((n_state * hist_rows[0], W_A), F32),
        jax.ShapeDtypeStruct((n_state * hist_rows[1], W_B), F32),
        jax.ShapeDtypeStruct((n_state * hist_rows[2], W_C), F32),
        jax.ShapeDtypeStruct((n_state * nb, W_C), F32),
        jax.ShapeDtypeStruct((n_state * hist_rows[3], W_D), F32),
    )
    out_specs = (
        pl.BlockSpec((r, D_MODEL), lambda i: (i, 0)),
        pl.BlockSpec((hist_rows[0], W_A), chunk_map),
        pl.BlockSpec((hist_rows[1], W_B), chunk_map),
        pl.BlockSpec((hist_rows[2], W_C), chunk_map),
        pl.BlockSpec((nb, W_C), chunk_map),
        pl.BlockSpec((hist_rows[3], W_D), chunk_map),
    )
    scratch = [pltpu.VMEM((hr + r, w), F32) for hr, w in zip(hist_rows, widths)]
    scratch += [
        pltpu.VMEM((nb, W_C), F32),
        pltpu.VMEM((r, W_C), F32),
        pltpu.VMEM((r, W_C), F32),
        pltpu.VMEM((r, W_C), F32),
        pltpu.VMEM((r, W_B), F32),
        pltpu.VMEM((r, 4 * D_MODEL), F32),
    ]
    return pl.pallas_call(
        functools.partial(_mixer_kernel, nb, tt, carry, offset),
        grid=(steps,),
        in_specs=in_specs,
        out_specs=out_specs,
        out_shape=out_shape,
        scratch_shapes=scratch,
        compiler_params=pltpu.CompilerParams(
            dimension_semantics=("arbitrary",), vmem_limit_bytes=VMEM_LIMIT),
        name="mixer_carry" if carry else "mixer_batch",
    )(u, x, ha, hb, hc, h0, hd, *params)


N_TOP = PEER_TOPK + 1
SV_ROWS = 24


PRO_LANES = 2 * LANES
SUBLANES = 8
MXU_COLS = 256


def _batcher_pairs(n):
    pairs = []

    def merge(lo, hi, r):
        step = r * 2
        if step < hi - lo:
            merge(lo, hi, step)
            merge(lo + r, hi, step)
            pairs.extend((i, i + r) for i in range(lo + r, hi - r, step))
        else:
            pairs.append((lo, lo + r))

    def sort(lo, hi):
        if hi - lo >= 1:
            mid = lo + (hi - lo) // 2
            sort(lo, mid)
            sort(mid + 1, hi)
            merge(lo, hi, 1)

    sort(0, n - 1)
    return pairs


def _sort_desc(ws):
    n = 1
    while n < len(ws):
        n *= 2
    ws = list(ws) + [None] * (n - len(ws))
    for i, j in _batcher_pairs(n):
        a, b = ws[i], ws[j]
        if b is None:
            continue
        if a is None:
            ws[i], ws[j] = b, None
        else:
            ws[i], ws[j] = jnp.maximum(a, b), jnp.minimum(a, b)
    return [w for w in ws if w is not None]


def _top_rows(ws, n):
    out = []
    for k in range(n):
        m = jnp.max(ws[0], axis=0, keepdims=True)
        out.append(m)
        remaining = n - 1 - k
        if remaining == 0:
            break
        hit = ws[0] >= m
        ws = [jnp.where(hit, ws[i + 1] if i + 1 < len(ws) else NEG_INF, ws[i])
              for i in range(min(len(ws), remaining))]
    return out


def _peer_kernel(tb, eh, final,
                 x_ref, g_ref, wqt_ref, keys_ref, u_ref, vt_ref, fg_ref,
                 o_ref,
                 xnt_ref, qt_ref, s_ref, sv_ref, a_ref, c_ref, r2_ref, e2_ref, *piece_refs):
    npl = tb // PRO_LANES
    ncol = tb // MXU_COLS
    ht_refs = [piece_refs[half * ncol:(half + 1) * ncol] for half in range(2)]
    ct_refs = [piece_refs[(2 + half) * ncol:(3 + half) * ncol] for half in range(2)]
    acc_refs = piece_refs[4 * ncol:5 * ncol]
    j = pl.program_id(1)

    @pl.when(j == 0)
    def _():
        xn = _rms(x_ref[...], g_ref[...])
        xnt_ref[...] = xn.T.astype(BF16)
        qt_ref[...] = jnp.dot(wqt_ref[...], xnt_ref[...], preferred_element_type=F32).astype(BF16)
        for hc in range(2 * PEER_HEADS):
            s = jnp.dot(keys_ref[hc], qt_ref[hc * KEY_DIM:(hc + 1) * KEY_DIM, :],
                        preferred_element_type=F32)
            for pt in range(npl):
                s_ref[hc, pt] = s[:, pt * PRO_LANES:(pt + 1) * PRO_LANES]

        def top_body(idx, carry):
            hc = idx // npl
            pt = idx % npl
            ws = _sort_desc([s_ref[hc, pt, i * SUBLANES:(i + 1) * SUBLANES, :] for i in range(N_KEYS // SUBLANES)])
            rows = _top_rows(ws, N_TOP)
            sv_ref[hc, pt] = jnp.full((SV_ROWS, PRO_LANES), NEG_INF, F32)
            for k in range(N_TOP):
                sv_ref[hc, pt, k:k + 1, :] = rows[k]
            return carry

        lax.fori_loop(0, 2 * PEER_HEADS * npl, top_body, 0)

        def thr_body(idx, carry):
            h = idx // npl
            pt = idx % npl
            sv1 = sv_ref[2 * h, pt]
            sv2 = sv_ref[2 * h + 1, pt]
            pieces = [sv1[0:1, :] + sv2[i * SUBLANES:(i + 1) * SUBLANES, :] for i in range(SV_ROWS // SUBLANES)]
            pieces += [sv1[a:a + 1, :] + sv2[0:SUBLANES, :] for a in range(1, SUBLANES)]
            pieces += [sv1[i * SUBLANES:(i + 1) * SUBLANES, :] + sv2[0:1, :] for i in range(1, SV_ROWS // SUBLANES)]
            best = _top_rows(_sort_desc(pieces), N_TOP)
            zsum = jnp.ones_like(best[0])
            for k in range(1, PEER_TOPK):
                zsum = zsum + jnp.exp(best[k] - best[0])
            thr = 0.5 * (best[PEER_TOPK - 1] + best[PEER_TOPK])
            s1 = s_ref[2 * h, pt]
            s2 = s_ref[2 * h + 1, pt]
            a_ref[h, pt] = jnp.exp(s1 - sv1[0:1, :]) / zsum
            e2_ref[h, pt] = jnp.exp(s2 - sv2[0:1, :]).astype(BF16)
            bthr = thr - s1
            rank2 = jnp.full(s2.shape, float(PEER_TOPK), F32)
            count = jnp.full(s1.shape, float(PEER_TOPK), F32)
            for k in range(PEER_TOPK - 1, -1, -1):
                rank2 = jnp.where(sv2[k:k + 1, :] <= s2, float(k), rank2)
                count = jnp.where(sv2[k:k + 1, :] < bthr, float(k), count)
            r2_ref[h, pt] = rank2.astype(BF16)
            c_ref[h, pt] = count
            return carry

        lax.fori_loop(0, PEER_HEADS * npl, thr_body, 0)
        for c in range(ncol):
            acc_refs[c][...] = jnp.zeros_like(acc_refs[c])

    def pre_activations(half, c):
        cols = slice(c * MXU_COLS, (c + 1) * MXU_COLS)
        ht_refs[half][c][...] = jnp.dot(u_ref[half * eh:(half + 1) * eh, :], xnt_ref[:, cols],
                                        preferred_element_type=F32).astype(BF16)

    def output_term(v, half, c):
        acc_refs[c][...] += jnp.dot(v, ct_refs[half][c][...], preferred_element_type=F32)

    def gates(half, c):
        for rr in range(eh // N_KEYS):
            i1 = (2 * j + half) * (eh // N_KEYS) + rr
            er = slice(rr * N_KEYS, (rr + 1) * N_KEYS)
            for l in range(MXU_COLS // LANES):
                lt = c * (MXU_COLS // LANES) + l
                pt = lt // 2
                ln = slice((lt % 2) * LANES, (lt % 2 + 1) * LANES)
                tl = slice(l * LANES, (l + 1) * LANES)
                g = jnp.zeros((N_KEYS, LANES), BF16)
                zero = jnp.zeros((N_KEYS, LANES), BF16)
                for h in range(PEER_HEADS):
                    a_row = jnp.broadcast_to(a_ref[h, pt, pl.ds(i1, 1), :][:, ln], (N_KEYS, LANES)).astype(BF16)
                    c_row = jnp.broadcast_to(c_ref[h, pt, pl.ds(i1, 1), :][:, ln], (N_KEYS, LANES)).astype(BF16)
                    g = g + jnp.where(r2_ref[h, pt, :, ln] < c_row, e2_ref[h, pt, :, ln], zero) * a_row
                ct_refs[half][c][er, tl] = g

    def activate(half, c):
        ct_refs[half][c][...] = ct_refs[half][c][...] * _gelu_sigmoid_form(ht_refs[half][c][...])

    for half in range(2):
        for c in range(ncol):
            gates(half, c)
            pre_activations(half, c)
    for half in range(2):
        for c in range(ncol):
            activate(half, c)
            output_term(vt_ref[half], half, c)

    @pl.when(j == pl.num_programs(1) - 1)
    def _():
        for c in range(ncol):
            rows = slice(c * MXU_COLS, (c + 1) * MXU_COLS)
            y = x_ref[rows, :] + acc_refs[c][...].T
            if final:
                y = _rms(y, fg_ref[...])
            o_ref[rows, :] = y


def _peer(x, g, wqt, keys, u_tab, vt_tiles, final_g, final, tb, eh):
    n = x.shape[0]
    npl = tb // PRO_LANES
    ncol = tb // MXU_COLS
    return pl.pallas_call(
        functools.partial(_peer_kernel, tb, eh, final),
        grid=(n // tb, N_EXPERTS // (2 * eh)),
        in_specs=[
            pl.BlockSpec((tb, D_MODEL), lambda i, j: (i, 0)),
            pl.BlockSpec((1, D_MODEL), lambda i, j: (0, 0)),
            pl.BlockSpec(wqt.shape, lambda i, j: (0, 0)),
            pl.BlockSpec(keys.shape, lambda i, j: (0, 0, 0)),
            pl.BlockSpec((2 * eh, D_MODEL), lambda i, j: (j, 0)),
            pl.BlockSpec((2, D_MODEL, eh), lambda i, j: (j, 0, 0)),
            pl.BlockSpec((1, D_MODEL), lambda i, j: (0, 0)),
        ],
        out_specs=pl.BlockSpec((tb, D_MODEL), lambda i, j: (i, 0)),
        out_shape=jax.ShapeDtypeStruct((n, D_MODEL), F32),
        scratch_shapes=[
            pltpu.VMEM((D_MODEL, tb), BF16),
            pltpu.VMEM((2 * PEER_HEADS * KEY_DIM, tb), BF16),
            pltpu.VMEM((2 * PEER_HEADS, npl, N_KEYS, PRO_LANES), F32),
            pltpu.VMEM((2 * PEER_HEADS, npl, SV_ROWS, PRO_LANES), F32),
            pltpu.VMEM((PEER_HEADS, npl, N_KEYS, PRO_LANES), F32),
            pltpu.VMEM((PEER_HEADS, npl, N_KEYS, PRO_LANES), F32),
            pltpu.VMEM((PEER_HEADS, npl, N_KEYS, PRO_LANES), BF16),
            pltpu.VMEM((PEER_HEADS, npl, N_KEYS, PRO_LANES), BF16),
        ] + [pltpu.VMEM((eh, MXU_COLS), BF16) for _ in range(4 * ncol)]
          + [pltpu.VMEM((D_MODEL, MXU_COLS), F32) for _ in range(ncol)],
        compiler_params=pltpu.CompilerParams(
            dimension_semantics=("arbitrary", "arbitrary"), vmem_limit_bytes=VMEM_LIMIT),
        name="peer_final" if final else "peer",
    )(x, g, wqt, keys, u_tab, vt_tiles, final_g)


PROMPT_NB = 8
PROMPT_TT = 32
SAMPLE_NB = 32
PEER_TB = 512
PEER_EH = 1024


def _to_chunks(a, nb):
    b, t, c = a.shape
    return a.reshape(b // nb, nb, t, c).transpose(0, 2, 1, 3).reshape(b * t, c)


def _from_chunks(a, nb, b, t):
    c = a.shape[-1]
    return a.reshape(b // nb, t, nb, c).transpose(0, 2, 1, 3).reshape(b, t, c)


def kernel(x_prompt, x_sample, state_conv_a, state_pool, state_conv_c, state_lru_h, state_conv_d, norm1_g, w_in, b_gate, conv_a_w, pool_w, pool_scale, conv_c_w, conv_c_b, lru_wa, lru_ba, lru_wi, lru_bi, lru_lambda, conv_d_w, conv_d_b, ln_d_g, ln_d_b, w_pa, w_pb, w_pc, w_pd, w_out, norm2_g, peer_wq, peer_keys, peer_u, peer_v, final_g):
    bp, tp, _ = x_prompt.shape
    bs, ts, _ = x_sample.shape
    xp = _to_chunks(x_prompt, PROMPT_NB)
    xs = _to_chunks(x_sample, SAMPLE_NB)
    fg = final_g.reshape(1, D_MODEL)

    outs_p = [[] for _ in range(5)]
    outs_s = [[] for _ in range(5)]
    for l in range(DEPTH):
        lp = {
            'conv_a_w': conv_a_w[l], 'pool_w': pool_w[l].astype(BF16), 'pool_scale': pool_scale[l].reshape(1, W_B),
            'conv_c_w': conv_c_w[l], 'conv_c_b': conv_c_b[l].reshape(1, W_C),
            'lru_wa': lru_wa[l].astype(BF16), 'lru_ba': lru_ba[l].reshape(1, W_C),
            'lru_wi': lru_wi[l].astype(BF16), 'lru_bi': lru_bi[l].reshape(1, W_C),
            'lru_lambda': lru_lambda[l].reshape(1, W_C),
            'conv_d_w': conv_d_w[l], 'conv_d_b': conv_d_b[l].reshape(1, W_D),
            'ln_d_g': ln_d_g[l].reshape(1, W_D), 'ln_d_b': ln_d_b[l].reshape(1, W_D),
            'b_gate': b_gate[l].reshape(1, 4 * D_MODEL),
            'w_pa': w_pa[l].astype(BF16), 'w_pb': w_pb[l].astype(BF16), 'w_pc': w_pc[l].astype(BF16),
            'w_pd': w_pd[l].astype(BF16), 'w_out': w_out[l].astype(BF16),
        }
        g1 = norm1_g[l].reshape(1, D_MODEL)
        g2 = norm2_g[l].reshape(1, D_MODEL)
        w_in_b = w_in[l].astype(BF16)
        wqt = peer_wq[l].T.astype(BF16)
        keys = peer_keys[l].reshape(2 * PEER_HEADS, N_KEYS, KEY_DIM).astype(BF16)
        u_tab = peer_u[l].astype(BF16)
        vt_tab = peer_v[l].astype(BF16).reshape(N_EXPERTS // PEER_EH, PEER_EH, D_MODEL).transpose(0, 2, 1)
        final = l == DEPTH - 1

        zero_hist = (jnp.zeros(((CONV_A - 1) * bp, W_A), F32), jnp.zeros((POOL_BUF * bp, W_B), F32),
                     jnp.zeros(((CONV_C - 1) * bp, W_C), F32), jnp.zeros((bp, W_C), F32),
                     jnp.zeros(((CONV_D - 1) * bp, W_D), F32))
        u = _inproj(xp, g1, w_in_b, 1024, 1536)
        xp, *st = _mixer(u, xp, zero_hist, lp, PROMPT_NB, PROMPT_TT, True, 0)
        for acc, s in zip(outs_p, st):
            acc.append(s)
        xp = _peer(xp, g2, wqt, keys, u_tab, vt_tab, fg, final, PEER_TB, PEER_EH)

        hist = (_to_chunks(state_conv_a[l], SAMPLE_NB), _to_chunks(state_pool[l], SAMPLE_NB),
                _to_chunks(state_conv_c[l], SAMPLE_NB), state_lru_h[l], _to_chunks(state_conv_d[l], SAMPLE_NB))
        u = _inproj(xs, g1, w_in_b, 512, 1536)
        xs, *st = _mixer(u, xs, hist, lp, SAMPLE_NB, ts, False, PAST_LEN)
        for acc, s in zip(outs_s, st):
            acc.append(s)
        xs = _peer(xs, g2, wqt, keys, u_tab, vt_tab, fg, final, PEER_TB, PEER_EH)

    y_prompt = _from_chunks(xp, PROMPT_NB, bp, tp)
    y_sample = _from_chunks(xs, SAMPLE_NB, bs, ts)

    def states(outs, nb, b):
        res = []
        for idx, hist_len in ((0, CONV_A - 1), (1, POOL_BUF), (2, CONV_C - 1), (3, None), (4, CONV_D - 1)):
            if hist_len is None:
                res.append(jnp.stack(outs[idx]))
            else:
                res.append(jnp.stack([_from_chunks(s, nb, b, hist_len) for s in outs[idx]]))
        return res

    ca_p, pool_p, cc_p, h_p, cd_p = states(outs_p, PROMPT_NB, bp)
    ca_s, pool_s, cc_s, h_s, cd_s = states(outs_s, SAMPLE_NB, bs)
    return (y_prompt, y_sample, ca_p, ca_s, pool_p, pool_s, cc_p, cc_s, h_p, h_s, cd_p, cd_s)
```

```python
import functools

import jax
import jax.numpy as jnp
from jax import lax
from jax.experimental import pallas as pl
from jax.experimental.pallas import tpu as pltpu

F32 = jnp.float32
BF16 = jnp.bfloat16

D_MODEL = 1024
DEPTH = 2
PAST_LEN = 16384
W_A = 512
CONV_A = 3
W_B = 512
POOL_WINDOWS = (2, 4, 8, 16)
POOL_GROUP = 128
POOL_BUF = 15
W_C = 1024
LRU_HEADS = 8
LRU_HEAD_DIM = 128
CONV_C = 4
LRU_C = 8.0
W_D = 512
CONV_D = 31
OFF_B = 3 * W_A
OFF_C = OFF_B + W_B
OFF_D = OFF_C + 2 * W_C
OFF_G = OFF_D + 2 * W_D
IN_TOTAL = OFF_G + 4 * D_MODEL
PEER_HEADS = 8
N_KEYS = 128
N_EXPERTS = N_KEYS * N_KEYS
KEY_DIM = 128
PEER_TOPK = 16
EPS = 1e-6

LANES = 128
VMEM_LIMIT = 56 * 1024 * 1024

GELU_C0 = 0.7978845608028654
GELU_C1 = 0.044715
NEG_INF = float("-inf")


def _sigmoid(x):
    return 1.0 / (1.0 + jnp.exp(-x))


def _gelu(x):
    return 0.5 * x * (1.0 + jnp.tanh(GELU_C0 * (x + GELU_C1 * (x * x * x))))


def _gelu_sigmoid_form(x):
    q = x * (x * x * (-2.0 * GELU_C0 * GELU_C1) + (-2.0 * GELU_C0))
    return x / (1.0 + jnp.exp(q))


def _rms(x, g):
    ms = jnp.mean(x * x, axis=-1, keepdims=True)
    return x * lax.rsqrt(ms + EPS) * g


def _bdot(a, b):
    return jnp.dot(a.astype(BF16), b.astype(BF16), preferred_element_type=F32)


def _inproj_kernel(x_ref, g_ref, w_ref, u_ref, xn_ref):
    @pl.when(pl.program_id(1) == 0)
    def _():
        xn_ref[...] = _rms(x_ref[...], g_ref[...]).astype(BF16)

    u_ref[...] = jnp.dot(xn_ref[...], w_ref[...], preferred_element_type=F32)


def _inproj(x, g, w_bf16, tm, tn):
    n = x.shape[0]
    return pl.pallas_call(
        _inproj_kernel,
        grid=(n // tm, IN_TOTAL // tn),
        in_specs=[
            pl.BlockSpec((tm, D_MODEL), lambda i, j: (i, 0)),
            pl.BlockSpec((1, D_MODEL), lambda i, j: (0, 0)),
            pl.BlockSpec((D_MODEL, tn), lambda i, j: (0, j)),
        ],
        out_specs=pl.BlockSpec((tm, tn), lambda i, j: (i, j)),
        out_shape=jax.ShapeDtypeStruct((n, IN_TOTAL), F32),
        scratch_shapes=[pltpu.VMEM((tm, D_MODEL), BF16)],
        compiler_params=pltpu.CompilerParams(
            dimension_semantics=("arbitrary", "arbitrary"), vmem_limit_bytes=VMEM_LIMIT),
        name="inproj",
    )(x, g, w_bf16)


def _mixer_kernel(nb, tt, carry, offset,
                  u_ref, x_ref, ha_ref, hb_ref, hc_ref, h0_ref, hd_ref,
                  caw_ref, pw_ref, ps_ref, ccw_ref, ccb_ref, wa_ref, ba_ref, wi_ref, bi_ref, lam_ref,
                  cdw_ref, cdb_ref, lng_ref, lnb_ref, bg_ref,
                  wpa_ref, wpb_ref, wpc_ref, wpd_ref, wout_ref,
                  xo_ref, sa_ref, sb_ref, sc_ref, hl_ref, sd_ref,
                  ea_ref, eb_ref, ec_ref, ed_ref, h_ref, a_ref, bx_ref, hs_ref, yb_ref, gate_ref):
    r = tt * nb
    step = pl.program_id(0)
    first = (step == 0) if carry else None

    def load_hist():
        ea_ref[0:(CONV_A - 1) * nb, :] = ha_ref[...]
        eb_ref[0:POOL_BUF * nb, :] = hb_ref[...]
        ec_ref[0:(CONV_C - 1) * nb, :] = hc_ref[...]
        ed_ref[0:(CONV_D - 1) * nb, :] = hd_ref[...]
        h_ref[...] = h0_ref[...]

    if carry:
        pl.when(first)(load_hist)
    else:
        load_hist()

    z = u_ref[:, W_A:2 * W_A] * u_ref[:, 2 * W_A:3 * W_A]
    ea_ref[(CONV_A - 1) * nb:, :] = z
    y = caw_ref[0:1, :] * ea_ref[0:r, :]
    for k in range(1, CONV_A):
        y = y + caw_ref[k:k + 1, :] * ea_ref[k * nb:k * nb + r, :]
    ya = u_ref[:, 0:W_A] * y

    eb_ref[POOL_BUF * nb:, :] = u_ref[:, OFF_B:OFF_C]
    row = lax.broadcasted_iota(jnp.int32, (r, POOL_GROUP), 0)
    tpos = (row >> (nb.bit_length() - 1)) + (step * tt if carry else 0) + offset
    for g, w in enumerate(POOL_WINDOWS):
        sl = slice(g * POOL_GROUP, (g + 1) * POOL_GROUP)
        acc = eb_ref[POOL_BUF * nb:POOL_BUF * nb + r, sl]
        for k in range(1, w):
            acc = acc + eb_ref[(POOL_BUF - k) * nb:(POOL_BUF - k) * nb + r, sl]
        cnt = jnp.minimum(tpos + 1, w).astype(F32)
        p = acc / cnt - eb_ref[POOL_BUF * nb:POOL_BUF * nb + r, sl]
        yb_ref[:, sl] = _bdot(p, pw_ref[g]) * ps_ref[:, sl]

    ec_ref[(CONV_C - 1) * nb:, :] = u_ref[:, OFF_C:OFF_C + W_C]
    xconv = ccw_ref[0:1, :] * ec_ref[0:r, :]
    for k in range(1, CONV_C):
        xconv = xconv + ccw_ref[k:k + 1, :] * ec_ref[k * nb:k * nb + r, :]
    xconv = xconv + ccb_ref[...]
    lam = lam_ref[...]
    nsp = -LRU_C * (jnp.maximum(-lam, 0.0) + jnp.log(1.0 + jnp.exp(-jnp.abs(lam))))
    for hh in range(LRU_HEADS):
        sl = slice(hh * LRU_HEAD_DIM, (hh + 1) * LRU_HEAD_DIM)
        xh = xconv[:, sl]
        xhb = xh.astype(BF16)
        rg = _sigmoid(jnp.dot(xhb, wa_ref[hh], preferred_element_type=F32) + ba_ref[:, sl])
        ig = _sigmoid(jnp.dot(xhb, wi_ref[hh], preferred_element_type=F32) + bi_ref[:, sl])
        log_a = rg * nsp[:, sl]
        a = jnp.exp(log_a)
        a_ref[:, sl] = a
        bx_ref[:, sl] = jnp.sqrt(1.0 - a * a) * (ig * xh)

    if tt <= 8:
        for t in range(tt):
            h = a_ref[t * nb:(t + 1) * nb, :] * h_ref[...] + bx_ref[t * nb:(t + 1) * nb, :]
            h_ref[...] = h
            hs_ref[t * nb:(t + 1) * nb, :] = h
    else:
        def scan_body(t, h):
            rows = pl.ds(pl.multiple_of(t * nb, nb), nb)
            h = a_ref[rows, :] * h + bx_ref[rows, :]
            hs_ref[rows, :] = h
            return h
        h_ref[...] = lax.fori_loop(0, tt, scan_body, h_ref[...])
    yc = hs_ref[...] * _gelu(u_ref[:, OFF_C + W_C:OFF_D])

    ed_ref[(CONV_D - 1) * nb:, :] = u_ref[:, OFF_D:OFF_D + W_D] * _sigmoid(u_ref[:, OFF_D + W_D:OFF_G])
    c = cdw_ref[0:1, :] * ed_ref[0:r, :]
    for k in range(1, CONV_D):
        c = c + cdw_ref[k:k + 1, :] * ed_ref[k * nb:k * nb + r, :]
    c = c + cdb_ref[...]
    mu = jnp.mean(c, axis=-1, keepdims=True)
    cc = c - mu
    var = jnp.mean(cc * cc, axis=-1, keepdims=True)
    ln = cc * lax.rsqrt(var + EPS) * lng_ref[...] + lnb_ref[...]
    yd = ln * _sigmoid(ln)

    gate_ref[...] = _sigmoid(u_ref[:, OFF_G:] + bg_ref[...])
    m = gate_ref[:, 0:D_MODEL] * _bdot(ya, wpa_ref[...])
    m = m + gate_ref[:, D_MODEL:2 * D_MODEL] * _bdot(yb_ref[...], wpb_ref[...])
    m = m + gate_ref[:, 2 * D_MODEL:3 * D_MODEL] * _bdot(yc, wpc_ref[...])
    m = m + gate_ref[:, 3 * D_MODEL:] * _bdot(yd, wpd_ref[...])
    xo_ref[...] = x_ref[...] + _bdot(m, wout_ref[...])

    def store_state():
        sa_ref[...] = ea_ref[r:r + (CONV_A - 1) * nb, :]
        sb_ref[...] = eb_ref[r:r + POOL_BUF * nb, :]
        sc_ref[...] = ec_ref[r:r + (CONV_C - 1) * nb, :]
        sd_ref[...] = ed_ref[r:r + (CONV_D - 1) * nb, :]
        hl_ref[...] = h_ref[...]

    if carry:
        pl.when(step == pl.num_programs(0) - 1)(store_state)
        for e_ref, hist in ((ea_ref, CONV_A - 1), (eb_ref, POOL_BUF), (ec_ref, CONV_C - 1), (ed_ref, CONV_D - 1)):
            total = hist * nb
            for s in range(0, total, r):
                n = min(r, total - s)
                e_ref[s:s + n, :] = e_ref[r + s:r + s + n, :]
    else:
        store_state()


def _mixer(u, x, hists, lp, nb, tt, carry, offset):
    n = x.shape[0]
    r = nb * tt
    steps = n // r
    ha, hb, hc, h0, hd = hists
    widths = (W_A, W_B, W_C, W_D)
    hist_rows = ((CONV_A - 1) * nb, POOL_BUF * nb, (CONV_C - 1) * nb, (CONV_D - 1) * nb)

    def chunk_map(i):
        return (0, 0) if carry else (i, 0)

    def const2(i):
        return (0, 0)

    def const3(i):
        return (0, 0, 0)

    def full(a):
        return pl.BlockSpec(a.shape, const2 if a.ndim == 2 else const3)

    params = [lp['conv_a_w'], lp['pool_w'], lp['pool_scale'], lp['conv_c_w'], lp['conv_c_b'],
              lp['lru_wa'], lp['lru_ba'], lp['lru_wi'], lp['lru_bi'], lp['lru_lambda'],
              lp['conv_d_w'], lp['conv_d_b'], lp['ln_d_g'], lp['ln_d_b'], lp['b_gate'],
              lp['w_pa'], lp['w_pb'], lp['w_pc'], lp['w_pd'], lp['w_out']]
    in_specs = [
        pl.BlockSpec((r, IN_TOTAL), lambda i: (i, 0)),
        pl.BlockSpec((r, D_MODEL), lambda i: (i, 0)),
        pl.BlockSpec((hist_rows[0], W_A), chunk_map),
        pl.BlockSpec((hist_rows[1], W_B), chunk_map),
        pl.BlockSpec((hist_rows[2], W_C), chunk_map),
        pl.BlockSpec((nb, W_C), chunk_map),
        pl.BlockSpec((hist_rows[3], W_D), chunk_map),
    ] + [full(p) for p in params]
    n_state = 1 if carry else steps
    out_shape = (
        jax.ShapeDtypeStruct((n, D_MODEL), F32),
        jax.ShapeDtypeStruct((n_state * hist_rows[0], W_A), F32),
        jax.ShapeDtypeStruct((n_state * hist_rows[1], W_B), F32),
        jax.ShapeDtypeStruct((n_state * hist_rows[2], W_C), F32),
        jax.ShapeDtypeStruct((n_state * nb, W_C), F32),
        jax.ShapeDtypeStruct((n_state * hist_rows[3], W_D), F32),
    )
    out_specs = (
        pl.BlockSpec((r, D_MODEL), lambda i: (i, 0)),
        pl.BlockSpec((hist_rows[0], W_A), chunk_map),
        pl.BlockSpec((hist_rows[1], W_B), chunk_map),
        pl.BlockSpec((hist_rows[2], W_C), chunk_map),
        pl.BlockSpec((nb, W_C), chunk_map),
        pl.BlockSpec((hist_rows[3], W_D), chunk_map),
    )
    scratch = [pltpu.VMEM((hr + r, w), F32) for hr, w in zip(hist_rows, widths)]
    scratch += [
        pltpu.VMEM((nb, W_C), F32),
        pltpu.VMEM((r, W_C), F32),
        pltpu.VMEM((r, W_C), F32),
        pltpu.VMEM((r, W_C), F32),
        pltpu.VMEM((r, W_B), F32),
        pltpu.VMEM((r, 4 * D_MODEL), F32),
    ]
    return pl.pallas_call(
        functools.partial(_mixer_kernel, nb, tt, carry, offset),
        grid=(steps,),
        in_specs=in_specs,
        out_specs=out_specs,
        out_shape=out_shape,
        scratch_shapes=scratch,
        compiler_params=pltpu.CompilerParams(
            dimension_semantics=("arbitrary",), vmem_limit_bytes=VMEM_LIMIT),
        name="mixer_carry" if carry else "mixer_batch",
    )(u, x, ha, hb, hc, h0, hd, *params)


N_TOP = PEER_TOPK + 1
SV_ROWS = 24


PRO_LANES = 2 * LANES
SUBLANES = 8
MXU_COLS = 256


def _batcher_pairs(n):
    pairs = []

    def merge(lo, hi, r):
        step = r * 2
        if step < hi - lo:
            merge(lo, hi, step)
            merge(lo + r, hi, step)
            pairs.extend((i, i + r) for i in range(lo + r, hi - r, step))
        else:
            pairs.append((lo, lo + r))

    def sort(lo, hi):
        if hi - lo >= 1:
            mid = lo + (hi - lo) // 2
            sort(lo, mid)
            sort(mid + 1, hi)
            merge(lo, hi, 1)

    sort(0, n - 1)
    return pairs


def _sort_desc(ws):
    n = 1
    while n < len(ws):
        n *= 2
    ws = list(ws) + [None] * (n - len(ws))
    for i, j in _batcher_pairs(n):
        a, b = ws[i], ws[j]
        if b is None:
            continue
        if a is None:
            ws[i], ws[j] = b, None
        else:
            ws[i], ws[j] = jnp.maximum(a, b), jnp.minimum(a, b)
    return [w for w in ws if w is not None]


def _top_rows(ws, n):
    out = []
    for k in range(n):
        m = jnp.max(ws[0], axis=0, keepdims=True)
        out.append(m)
        remaining = n - 1 - k
        if remaining == 0:
            break
        hit = ws[0] >= m
        ws = [jnp.where(hit, ws[i + 1] if i + 1 < len(ws) else NEG_INF, ws[i])
              for i in range(min(len(ws), remaining))]
    return out


def _peer_kernel(tb, ep, nparts, final, x_ref, g_ref, wqt_ref, keys_ref, *refs):
    npl = tb // PRO_LANES
    ncol = tb // MXU_COLS
    u_refs, vt_refs = refs[:nparts], refs[nparts:2 * nparts]
    fg_ref, o_ref, xnt_ref, qt_ref, s_ref, sv_ref, a_ref, c_ref, r2_ref, e2_ref = refs[2 * nparts:2 * nparts + 10]
    piece_refs = refs[2 * nparts + 10:]
    ht_refs = [piece_refs[p * ncol:(p + 1) * ncol] for p in range(nparts)]
    ct_refs = [piece_refs[(nparts + p) * ncol:(nparts + p + 1) * ncol] for p in range(nparts)]
    acc_refs = piece_refs[2 * nparts * ncol:(2 * nparts + 1) * ncol]
    j = pl.program_id(1)

    @pl.when(j == 0)
    def _():
        xn = _rms(x_ref[...], g_ref[...])
        xnt_ref[...] = xn.T.astype(BF16)
        qt_ref[...] = jnp.dot(wqt_ref[...], xnt_ref[...], preferred_element_type=F32).astype(BF16)
        for hc in range(2 * PEER_HEADS):
            s = jnp.dot(keys_ref[hc], qt_ref[hc * KEY_DIM:(hc + 1) * KEY_DIM, :],
                        preferred_element_type=F32)
            for pt in range(npl):
                s_ref[hc, pt] = s[:, pt * PRO_LANES:(pt + 1) * PRO_LANES]

        def top_body(idx, carry):
            hc = idx // npl
            pt = idx % npl
            ws = _sort_desc([s_ref[hc, pt, i * SUBLANES:(i + 1) * SUBLANES, :] for i in range(N_KEYS // SUBLANES)])
            rows = _top_rows(ws, N_TOP)
            sv_ref[hc, pt] = jnp.full((SV_ROWS, PRO_LANES), NEG_INF, F32)
            for k in range(N_TOP):
                sv_ref[hc, pt, k:k + 1, :] = rows[k]
            return carry

        lax.fori_loop(0, 2 * PEER_HEADS * npl, top_body, 0)

        def thr_body(idx, carry):
            h = idx // npl
            pt = idx % npl
            sv1 = sv_ref[2 * h, pt]
            sv2 = sv_ref[2 * h + 1, pt]
            pieces = [sv1[0:1, :] + sv2[i * SUBLANES:(i + 1) * SUBLANES, :] for i in range(SV_ROWS // SUBLANES)]
            pieces += [sv1[a:a + 1, :] + sv2[0:SUBLANES, :] for a in range(1, SUBLANES)]
            pieces += [sv1[i * SUBLANES:(i + 1) * SUBLANES, :] + sv2[0:1, :] for i in range(1, SV_ROWS // SUBLANES)]
            best = _top_rows(_sort_desc(pieces), N_TOP)
            zsum = jnp.ones_like(best[0])
            for k in range(1, PEER_TOPK):
                zsum = zsum + jnp.exp(best[k] - best[0])
            thr = 0.5 * (best[PEER_TOPK - 1] + best[PEER_TOPK])
            s1 = s_ref[2 * h, pt]
            s2 = s_ref[2 * h + 1, pt]
            a_ref[h, pt] = jnp.exp(s1 - sv1[0:1, :]) / zsum
            e2_ref[h, pt] = jnp.exp(s2 - sv2[0:1, :]).astype(BF16)
            bthr = thr - s1
            rank2 = jnp.full(s2.shape, float(PEER_TOPK), F32)
            count = jnp.full(s1.shape, float(PEER_TOPK), F32)
            for k in range(PEER_TOPK - 1, -1, -1):
                rank2 = jnp.where(sv2[k:k + 1, :] <= s2, float(k), rank2)
                count = jnp.where(sv2[k:k + 1, :] < bthr, float(k), count)
            r2_ref[h, pt] = rank2.astype(BF16)
            c_ref[h, pt] = count
            return carry

        lax.fori_loop(0, PEER_HEADS * npl, thr_body, 0)
        for c in range(ncol):
            acc_refs[c][...] = jnp.zeros_like(acc_refs[c])

    def pre_activations(half, c):
        cols = slice(c * MXU_COLS, (c + 1) * MXU_COLS)
        ht_refs[half][c][...] = jnp.dot(u_refs[half][...], xnt_ref[:, cols],
                                        preferred_element_type=F32).astype(BF16)

    def output_term(half, c):
        acc_refs[c][...] += jnp.dot(vt_refs[half][0], ct_refs[half][c][...], preferred_element_type=F32)

    def gates(half, c):
        for rr in range(ep // N_KEYS):
            i1 = (nparts * j + half) * (ep // N_KEYS) + rr
            er = slice(rr * N_KEYS, (rr + 1) * N_KEYS)
            for l in range(MXU_COLS // LANES):
                lt = c * (MXU_COLS // LANES) + l
                pt = lt // 2
                ln = slice((lt % 2) * LANES, (lt % 2 + 1) * LANES)
                tl = slice(l * LANES, (l + 1) * LANES)
                g = jnp.zeros((N_KEYS, LANES), BF16)
                zero = jnp.zeros((N_KEYS, LANES), BF16)
                for h in range(PEER_HEADS):
                    a_row = jnp.broadcast_to(a_ref[h, pt, pl.ds(i1, 1), :][:, ln], (N_KEYS, LANES)).astype(BF16)
                    c_row = jnp.broadcast_to(c_ref[h, pt, pl.ds(i1, 1), :][:, ln], (N_KEYS, LANES)).astype(BF16)
                    g = g + jnp.where(r2_ref[h, pt, :, ln] < c_row, e2_ref[h, pt, :, ln], zero) * a_row
                ct_refs[half][c][er, tl] = g

    def activate(half, c):
        ct_refs[half][c][...] = ct_refs[half][c][...] * _gelu_sigmoid_form(ht_refs[half][c][...])

    for half in range(nparts):
        for c in range(ncol):
            gates(half, c)
            pre_activations(half, c)
    for half in range(nparts):
        for c in range(ncol):
            activate(half, c)
            output_term(half, c)

    @pl.when(j == pl.num_programs(1) - 1)
    def _():
        for c in range(ncol):
            rows = slice(c * MXU_COLS, (c + 1) * MXU_COLS)
            y = x_ref[rows, :] + acc_refs[c][...].T
            if final:
                y = _rms(y, fg_ref[...])
            o_ref[rows, :] = y


def _peer(x, g, wqt, keys, u_tab, vt_tiles, final_g, final, tb, ep, nparts):
    n = x.shape[0]
    npl = tb // PRO_LANES
    ncol = tb // MXU_COLS

    def part_spec(shape, p):
        return pl.BlockSpec(shape, lambda i, j: (nparts * j + p,) + (0,) * (len(shape) - 1))

    return pl.pallas_call(
        functools.partial(_peer_kernel, tb, ep, nparts, final),
        grid=(n // tb, N_EXPERTS // (nparts * ep)),
        in_specs=[
            pl.BlockSpec((tb, D_MODEL), lambda i, j: (i, 0)),
            pl.BlockSpec((1, D_MODEL), lambda i, j: (0, 0)),
            pl.BlockSpec(wqt.shape, lambda i, j: (0, 0)),
            pl.BlockSpec(keys.shape, lambda i, j: (0, 0, 0)),
        ] + [part_spec((ep, D_MODEL), p) for p in range(nparts)]
          + [part_spec((1, D_MODEL, ep), p) for p in range(nparts)]
          + [pl.BlockSpec((1, D_MODEL), lambda i, j: (0, 0))],
        out_specs=pl.BlockSpec((tb, D_MODEL), lambda i, j: (i, 0)),
        out_shape=jax.ShapeDtypeStruct((n, D_MODEL), F32),
        scratch_shapes=[
            pltpu.VMEM((D_MODEL, tb), BF16),
            pltpu.VMEM((2 * PEER_HEADS * KEY_DIM, tb), BF16),
            pltpu.VMEM((2 * PEER_HEADS, npl, N_KEYS, PRO_LANES), F32),
            pltpu.VMEM((2 * PEER_HEADS, npl, SV_ROWS, PRO_LANES), F32),
            pltpu.VMEM((PEER_HEADS, npl, N_KEYS, PRO_LANES), F32),
            pltpu.VMEM((PEER_HEADS, npl, N_KEYS, PRO_LANES), F32),
            pltpu.VMEM((PEER_HEADS, npl, N_KEYS, PRO_LANES), BF16),
            pltpu.VMEM((PEER_HEADS, npl, N_KEYS, PRO_LANES), BF16),
        ] + [pltpu.VMEM((ep, MXU_COLS), BF16) for _ in range(2 * nparts * ncol)]
          + [pltpu.VMEM((D_MODEL, MXU_COLS), F32) for _ in range(ncol)],
        compiler_params=pltpu.CompilerParams(
            dimension_semantics=("arbitrary", "arbitrary"), vmem_limit_bytes=VMEM_LIMIT),
        name="peer_final" if final else "peer",
    )(x, g, wqt, keys, *([u_tab] * nparts), *([vt_tiles] * nparts), final_g)


PROMPT_NB = 8
PROMPT_TT = 32
SAMPLE_NB = 32
PEER_TB = 512
PEER_EP = 512
PEER_PARTS = 4


def _to_chunks(a, nb):
    b, t, c = a.shape
    return a.reshape(b // nb, nb, t, c).transpose(0, 2, 1, 3).reshape(b * t, c)


def _from_chunks(a, nb, b, t):
    c = a.shape[-1]
    return a.reshape(b // nb, t, nb, c).transpose(0, 2, 1, 3).reshape(b, t, c)


def kernel(x_prompt, x_sample, state_conv_a, state_pool, state_conv_c, state_lru_h, state_conv_d, norm1_g, w_in, b_gate, conv_a_w, pool_w, pool_scale, conv_c_w, conv_c_b, lru_wa, lru_ba, lru_wi, lru_bi, lru_lambda, conv_d_w, conv_d_b, ln_d_g, ln_d_b, w_pa, w_pb, w_pc, w_pd, w_out, norm2_g, peer_wq, peer_keys, peer_u, peer_v, final_g):
    bp, tp, _ = x_prompt.shape
    bs, ts, _ = x_sample.shape
    xp = _to_chunks(x_prompt, PROMPT_NB)
    xs = _to_chunks(x_sample, SAMPLE_NB)
    fg = final_g.reshape(1, D_MODEL)

    outs_p = [[] for _ in range(5)]
    outs_s = [[] for _ in range(5)]
    for l in range(DEPTH):
        lp = {
            'conv_a_w': conv_a_w[l], 'pool_w': pool_w[l].astype(BF16), 'pool_scale': pool_scale[l].reshape(1, W_B),
            'conv_c_w': conv_c_w[l], 'conv_c_b': conv_c_b[l].reshape(1, W_C),
            'lru_wa': lru_wa[l].astype(BF16), 'lru_ba': lru_ba[l].reshape(1, W_C),
            'lru_wi': lru_wi[l].astype(BF16), 'lru_bi': lru_bi[l].reshape(1, W_C),
            'lru_lambda': lru_lambda[l].reshape(1, W_C),
            'conv_d_w': conv_d_w[l], 'conv_d_b': conv_d_b[l].reshape(1, W_D),
            'ln_d_g': ln_d_g[l].reshape(1, W_D), 'ln_d_b': ln_d_b[l].reshape(1, W_D),
            'b_gate': b_gate[l].reshape(1, 4 * D_MODEL),
            'w_pa': w_pa[l].astype(BF16), 'w_pb': w_pb[l].astype(BF16), 'w_pc': w_pc[l].astype(BF16),
            'w_pd': w_pd[l].astype(BF16), 'w_out': w_out[l].astype(BF16),
        }
        g1 = norm1_g[l].reshape(1, D_MODEL)
        g2 = norm2_g[l].reshape(1, D_MODEL)
        w_in_b = w_in[l].astype(BF16)
        wqt = peer_wq[l].T.astype(BF16)
        keys = peer_keys[l].reshape(2 * PEER_HEADS, N_KEYS, KEY_DIM).astype(BF16)
        u_tab = peer_u[l].astype(BF16)
        vt_tab = peer_v[l].astype(BF16).reshape(N_EXPERTS // PEER_EP, PEER_EP, D_MODEL).transpose(0, 2, 1)
        final = l == DEPTH - 1

        zero_hist = (jnp.zeros(((CONV_A - 1) * bp, W_A), F32), jnp.zeros((POOL_BUF * bp, W_B), F32),
                     jnp.zeros(((CONV_C - 1) * bp, W_C), F32), jnp.zeros((bp, W_C), F32),
                     jnp.zeros(((CONV_D - 1) * bp, W_D), F32))
        u = _inproj(xp, g1, w_in_b, 1024, 1536)
        xp, *st = _mixer(u, xp, zero_hist, lp, PROMPT_NB, PROMPT_TT, True, 0)
        for acc, s in zip(outs_p, st):
            acc.append(s)
        xp = _peer(xp, g2, wqt, keys, u_tab, vt_tab, fg, final, PEER_TB, PEER_EP, PEER_PARTS)

        hist = (_to_chunks(state_conv_a[l], SAMPLE_NB), _to_chunks(state_pool[l], SAMPLE_NB),
                _to_chunks(state_conv_c[l], SAMPLE_NB), state_lru_h[l], _to_chunks(state_conv_d[l], SAMPLE_NB))
        u = _inproj(xs, g1, w_in_b, 512, 1536)
        xs, *st = _mixer(u, xs, hist, lp, SAMPLE_NB, ts, False, PAST_LEN)
        for acc, s in zip(outs_s, st):
            acc.append(s)
        xs = _peer(xs, g2, wqt, keys, u_tab, vt_tab, fg, final, PEER_TB, PEER_EP, PEER_PARTS)

    y_prompt = _from_chunks(xp, PROMPT_NB, bp, tp)
    y_sample = _from_chunks(xs, SAMPLE_NB, bs, ts)

    def states(outs, nb, b):
        res = []
        for idx, hist_len in ((0, CONV_A - 1), (1, POOL_BUF), (2, CONV_C - 1), (3, None), (4, CONV_D - 1)):
            if hist_len is None:
                res.append(jnp.stack(outs[idx]))
            else:
                res.append(jnp.stack([_from_chunks(s, nb, b, hist_len) for s in outs[idx]]))
        return res

    ca_p, pool_p, cc_p, h_p, cd_p = states(outs_p, PROMPT_NB, bp)
    ca_s, pool_s, cc_s, h_s, cd_s = states(outs_s, SAMPLE_NB, bs)
    return (y_prompt, y_sample, ca_p, ca_s, pool_p, pool_s, cc_p, cc_s, h_p, h_s, cd_p, cd_s)
```

```python
import functools

import jax
import jax.numpy as jnp
from jax import lax
from jax.experimental import pallas as pl
from jax.experimental.pallas import tpu as pltpu

F32 = jnp.float32
BF16 = jnp.bfloat16

D_MODEL = 1024
DEPTH = 2
PAST_LEN = 16384
W_A = 512
CONV_A = 3
W_B = 512
POOL_WINDOWS = (2, 4, 8, 16)
POOL_GROUP = 128
POOL_BUF = 15
W_C = 1024
LRU_HEADS = 8
LRU_HEAD_DIM = 128
CONV_C = 4
LRU_C = 8.0
W_D = 512
CONV_D = 31
OFF_B = 3 * W_A
OFF_C = OFF_B + W_B
OFF_D = OFF_C + 2 * W_C
OFF_G = OFF_D + 2 * W_D
IN_TOTAL = OFF_G + 4 * D_MODEL
PEER_HEADS = 8
N_KEYS = 128
N_EXPERTS = N_KEYS * N_KEYS
KEY_DIM = 128
PEER_TOPK = 16
EPS = 1e-6

LANES = 128
VMEM_LIMIT = 56 * 1024 * 1024

GELU_C0 = 0.7978845608028654
GELU_C1 = 0.044715
LOG2_E = 1.4426950408889634
NEG_INF = float("-inf")


def _sigmoid(x):
    return 1.0 / (1.0 + jnp.exp(-x))


def _gelu(x):
    return 0.5 * x * (1.0 + jnp.tanh(GELU_C0 * (x + GELU_C1 * (x * x * x))))


def _gelu_sigmoid_form(x):
    q = x * (x * x * (-2.0 * GELU_C0 * GELU_C1 * LOG2_E) + (-2.0 * GELU_C0 * LOG2_E))
    return x / (1.0 + jnp.exp2(q))


def _rms(x, g):
    ms = jnp.mean(x * x, axis=-1, keepdims=True)
    return x * lax.rsqrt(ms + EPS) * g


def _bdot(a, b):
    return jnp.dot(a.astype(BF16), b.astype(BF16), preferred_element_type=F32)


def _inproj_kernel(x_ref, g_ref, w_ref, u_ref, xn_ref):
    @pl.when(pl.program_id(1) == 0)
    def _():
        xn_ref[...] = _rms(x_ref[...], g_ref[...]).astype(BF16)

    u_ref[...] = jnp.dot(xn_ref[...], w_ref[...], preferred_element_type=F32)


def _inproj(x, g, w_bf16, tm, tn):
    n = x.shape[0]
    return pl.pallas_call(
        _inproj_kernel,
        grid=(n // tm, IN_TOTAL // tn),
        in_specs=[
            pl.BlockSpec((tm, D_MODEL), lambda i, j: (i, 0)),
            pl.BlockSpec((1, D_MODEL), lambda i, j: (0, 0)),
            pl.BlockSpec((D_MODEL, tn), lambda i, j: (0, j)),
        ],
        out_specs=pl.BlockSpec((tm, tn), lambda i, j: (i, j)),
        out_shape=jax.ShapeDtypeStruct((n, IN_TOTAL), F32),
        scratch_shapes=[pltpu.VMEM((tm, D_MODEL), BF16)],
        compiler_params=pltpu.CompilerParams(
            dimension_semantics=("arbitrary", "arbitrary"), vmem_limit_bytes=VMEM_LIMIT),
        name="inproj",
    )(x, g, w_bf16)


def _mixer_kernel(nb, tt, carry, offset,
                  u_ref, x_ref, ha_ref, hb_ref, hc_ref, h0_ref, hd_ref,
                  caw_ref, pw_ref, ps_ref, ccw_ref, ccb_ref, wa_ref, ba_ref, wi_ref, bi_ref, lam_ref,
                  cdw_ref, cdb_ref, lng_ref, lnb_ref, bg_ref,
                  wpa_ref, wpb_ref, wpc_ref, wpd_ref, wout_ref,
                  xo_ref, sa_ref, sb_ref, sc_ref, hl_ref, sd_ref,
                  ea_ref, eb_ref, ec_ref, ed_ref, h_ref, a_ref, bx_ref, hs_ref, yb_ref, gate_ref):
    r = tt * nb
    step = pl.program_id(0)
    first = (step == 0) if carry else None

    def load_hist():
        ea_ref[0:(CONV_A - 1) * nb, :] = ha_ref[...]
        eb_ref[0:POOL_BUF * nb, :] = hb_ref[...]
        ec_ref[0:(CONV_C - 1) * nb, :] = hc_ref[...]
        ed_ref[0:(CONV_D - 1) * nb, :] = hd_ref[...]
        h_ref[...] = h0_ref[...]

    if carry:
        pl.when(first)(load_hist)
    else:
        load_hist()

    z = u_ref[:, W_A:2 * W_A] * u_ref[:, 2 * W_A:3 * W_A]
    ea_ref[(CONV_A - 1) * nb:, :] = z
    y = caw_ref[0:1, :] * ea_ref[0:r, :]
    for k in range(1, CONV_A):
        y = y + caw_ref[k:k + 1, :] * ea_ref[k * nb:k * nb + r, :]
    ya = u_ref[:, 0:W_A] * y

    eb_ref[POOL_BUF * nb:, :] = u_ref[:, OFF_B:OFF_C]
    row = lax.broadcasted_iota(jnp.int32, (r, POOL_GROUP), 0)
    tpos = (row >> (nb.bit_length() - 1)) + (step * tt if carry else 0) + offset
    for g, w in enumerate(POOL_WINDOWS):
        sl = slice(g * POOL_GROUP, (g + 1) * POOL_GROUP)
        acc = eb_ref[POOL_BUF * nb:POOL_BUF * nb + r, sl]
        for k in range(1, w):
            acc = acc + eb_ref[(POOL_BUF - k) * nb:(POOL_BUF - k) * nb + r, sl]
        cnt = jnp.minimum(tpos + 1, w).astype(F32)
        p = acc / cnt - eb_ref[POOL_BUF * nb:POOL_BUF * nb + r, sl]
        yb_ref[:, sl] = _bdot(p, pw_ref[g]) * ps_ref[:, sl]

    ec_ref[(CONV_C - 1) * nb:, :] = u_ref[:, OFF_C:OFF_C + W_C]
    xconv = ccw_ref[0:1, :] * ec_ref[0:r, :]
    for k in range(1, CONV_C):
        xconv = xconv + ccw_ref[k:k + 1, :] * ec_ref[k * nb:k * nb + r, :]
    xconv = xconv + ccb_ref[...]
    lam = lam_ref[...]
    nsp = -LRU_C * (jnp.maximum(-lam, 0.0) + jnp.log(1.0 + jnp.exp(-jnp.abs(lam))))
    for hh in range(LRU_HEADS):
        sl = slice(hh * LRU_HEAD_DIM, (hh + 1) * LRU_HEAD_DIM)
        xh = xconv[:, sl]
        xhb = xh.astype(BF16)
        rg = _sigmoid(jnp.dot(xhb, wa_ref[hh], preferred_element_type=F32) + ba_ref[:, sl])
        ig = _sigmoid(jnp.dot(xhb, wi_ref[hh], preferred_element_type=F32) + bi_ref[:, sl])
        log_a = rg * nsp[:, sl]
        a = jnp.exp(log_a)
        a_ref[:, sl] = a
        bx_ref[:, sl] = jnp.sqrt(1.0 - a * a) * (ig * xh)

    if tt <= 8:
        for t in range(tt):
            h = a_ref[t * nb:(t + 1) * nb, :] * h_ref[...] + bx_ref[t * nb:(t + 1) * nb, :]
            h_ref[...] = h
            hs_ref[t * nb:(t + 1) * nb, :] = h
    else:
        def scan_body(t, h):
            rows = pl.ds(pl.multiple_of(t * nb, nb), nb)
            h = a_ref[rows, :] * h + bx_ref[rows, :]
            hs_ref[rows, :] = h
            return h
        h_ref[...] = lax.fori_loop(0, tt, scan_body, h_ref[...])
    yc = hs_ref[...] * _gelu(u_ref[:, OFF_C + W_C:OFF_D])

    ed_ref[(CONV_D - 1) * nb:, :] = u_ref[:, OFF_D:OFF_D + W_D] * _sigmoid(u_ref[:, OFF_D + W_D:OFF_G])
    c = cdw_ref[0:1, :] * ed_ref[0:r, :]
    for k in range(1, CONV_D):
        c = c + cdw_ref[k:k + 1, :] * ed_ref[k * nb:k * nb + r, :]
    c = c + cdb_ref[...]
    mu = jnp.mean(c, axis=-1, keepdims=True)
    cc = c - mu
    var = jnp.mean(cc * cc, axis=-1, keepdims=True)
    ln = cc * lax.rsqrt(var + EPS) * lng_ref[...] + lnb_ref[...]
    yd = ln * _sigmoid(ln)

    gate_ref[...] = _sigmoid(u_ref[:, OFF_G:] + bg_ref[...])
    m = gate_ref[:, 0:D_MODEL] * _bdot(ya, wpa_ref[...])
    m = m + gate_ref[:, D_MODEL:2 * D_MODEL] * _bdot(yb_ref[...], wpb_ref[...])
    m = m + gate_ref[:, 2 * D_MODEL:3 * D_MODEL] * _bdot(yc, wpc_ref[...])
    m = m + gate_ref[:, 3 * D_MODEL:] * _bdot(yd, wpd_ref[...])
    xo_ref[...] = x_ref[...] + _bdot(m, wout_ref[...])

    def store_state():
        sa_ref[...] = ea_ref[r:r + (CONV_A - 1) * nb, :]
        sb_ref[...] = eb_ref[r:r + POOL_BUF * nb, :]
        sc_ref[...] = ec_ref[r:r + (CONV_C - 1) * nb, :]
        sd_ref[...] = ed_ref[r:r + (CONV_D - 1) * nb, :]
        hl_ref[...] = h_ref[...]

    if carry:
        pl.when(step == pl.num_programs(0) - 1)(store_state)
        for e_ref, hist in ((ea_ref, CONV_A - 1), (eb_ref, POOL_BUF), (ec_ref, CONV_C - 1), (ed_ref, CONV_D - 1)):
            total = hist * nb
            for s in range(0, total, r):
                n = min(r, total - s)
                e_ref[s:s + n, :] = e_ref[r + s:r + s + n, :]
    else:
        store_state()


def _mixer(u, x, hists, lp, nb, tt, carry, offset):
    n = x.shape[0]
    r = nb * tt
    steps = n // r
    ha, hb, hc, h0, hd = hists
    widths = (W_A, W_B, W_C, W_D)
    hist_rows = ((CONV_A - 1) * nb, POOL_BUF * nb, (CONV_C - 1) * nb, (CONV_D - 1) * nb)

    def chunk_map(i):
        return (0, 0) if carry else (i, 0)

    def const2(i):
        return (0, 0)

    def const3(i):
        return (0, 0, 0)

    def full(a):
        return pl.BlockSpec(a.shape, const2 if a.ndim == 2 else const3)

    params = [lp['conv_a_w'], lp['pool_w'], lp['pool_scale'], lp['conv_c_w'], lp['conv_c_b'],
              lp['lru_wa'], lp['lru_ba'], lp['lru_wi'], lp['lru_bi'], lp['lru_lambda'],
              lp['conv_d_w'], lp['conv_d_b'], lp['ln_d_g'], lp['ln_d_b'], lp['b_gate'],
              lp['w_pa'], lp['w_pb'], lp['w_pc'], lp['w_pd'], lp['w_out']]
    in_specs = [
        pl.BlockSpec((r, IN_TOTAL), lambda i: (i, 0)),
        pl.BlockSpec((r, D_MODEL), lambda i: (i, 0)),
        pl.BlockSpec((hist_rows[0], W_A), chunk_map),
        pl.BlockSpec((hist_rows[1], W_B), chunk_map),
        pl.BlockSpec((hist_rows[2], W_C), chunk_map),
        pl.BlockSpec((nb, W_C), chunk_map),
        pl.BlockSpec((hist_rows[3], W_D), chunk_map),
    ] + [full(p) for p in params]
    n_state = 1 if carry else steps
    out_shape = (
        jax.ShapeDtypeStruct((n, D_MODEL), F32),
        jax.ShapeDtypeStruct((n_state * hist_rows[0], W_A), F32),
        jax.ShapeDtypeStruct((n_state * hist_rows[1], W_B), F32),
        jax.ShapeDtypeStruct((n_state * hist_rows[2], W_C), F32),
        jax.ShapeDtypeStruct((n_state * nb, W_C), F32),
        jax.ShapeDtypeStruct((n_state * hist_rows[3], W_D), F32),
    )
    out_specs = (
        pl.BlockSpec((r, D_MODEL), lambda i: (i, 0)),
        pl.BlockSpec((hist_rows[0], W_A), chunk_map),
        pl.BlockSpec((hist_rows[1], W_B), chunk_map),
        pl.BlockSpec((hist_rows[2], W_C), chunk_map),
        pl.BlockSpec((nb, W_C), chunk_map),
        pl.BlockSpec((hist_rows[3], W_D), chunk_map),
    )
    scratch = [pltpu.VMEM((hr + r, w), F32) for hr, w in zip(hist_rows, widths)]
    scratch += [
        pltpu.VMEM((nb, W_C), F32),
        pltpu.VMEM((r, W_C), F32),
        pltpu.VMEM((r, W_C), F32),
        pltpu.VMEM((r, W_C), F32),
        pltpu.VMEM((r, W_B), F32),
        pltpu.VMEM((r, 4 * D_MODEL), F32),
    ]
    return pl.pallas_call(
        functools.partial(_mixer_kernel, nb, tt, carry, offset),
        grid=(steps,),
        in_specs=in_specs,
        out_specs=out_specs,
        out_shape=out_shape,
        scratch_shapes=scratch,
        compiler_params=pltpu.CompilerParams(
            dimension_semantics=("arbitrary",), vmem_limit_bytes=VMEM_LIMIT),
        name="mixer_carry" if carry else "mixer_batch",
    )(u, x, ha, hb, hc, h0, hd, *params)


N_TOP = PEER_TOPK + 1
SV_ROWS = 24


PRO_LANES = 2 * LANES
SUBLANES = 8
MXU_COLS = 256


def _batcher_pairs(n):
    pairs = []

    def merge(lo, hi, r):
        step = r * 2
        if step < hi - lo:
            merge(lo, hi, step)
            merge(lo + r, hi, step)
            pairs.extend((i, i + r) for i in range(lo + r, hi - r, step))
        else:
            pairs.append((lo, lo + r))

    def sort(lo, hi):
        if hi - lo >= 1:
            mid = lo + (hi - lo) // 2
            sort(lo, mid)
            sort(mid + 1, hi)
            merge(lo, hi, 1)

    sort(0, n - 1)
    return pairs


def _sort_desc(ws):
    n = 1
    while n < len(ws):
        n *= 2
    ws = list(ws) + [None] * (n - len(ws))
    for i, j in _batcher_pairs(n):
        a, b = ws[i], ws[j]
        if b is None:
            continue
        if a is None:
            ws[i], ws[j] = b, None
        else:
            ws[i], ws[j] = jnp.maximum(a, b), jnp.minimum(a, b)
    return [w for w in ws if w is not None]


def _top_rows(ws, n):
    out = []
    for k in range(n):
        m = jnp.max(ws[0], axis=0, keepdims=True)
        out.append(m)
        remaining = n - 1 - k
        if remaining == 0:
            break
        hit = ws[0] >= m
        ws = [jnp.where(hit, ws[i + 1] if i + 1 < len(ws) else NEG_INF, ws[i])
              for i in range(min(len(ws), remaining))]
    return out


def _peer_kernel(tb, et, ts, final,
                 x_ref, g_ref, wqt_ref, keys_ref, u_ref, vt_ref, vtp_ref, fg_ref,
                 o_ref,
                 xnt_ref, qt_ref, s_ref, sv_ref, a_ref, b_ref, e2_ref, ht_ref, ct_ref, acc_ref):
    npl = tb // PRO_LANES
    rows_per_tile = et // N_KEYS
    j = pl.program_id(1)

    @pl.when(j == 0)
    def _():
        xn = _rms(x_ref[...], g_ref[...])
        xnt_ref[...] = xn.T.astype(BF16)
        qt_ref[...] = jnp.dot(wqt_ref[...], xnt_ref[...], preferred_element_type=F32).astype(BF16)
        for hc in range(2 * PEER_HEADS):
            s = jnp.dot(keys_ref[hc], qt_ref[hc * KEY_DIM:(hc + 1) * KEY_DIM, :],
                        preferred_element_type=F32)
            for pt in range(npl):
                s_ref[hc, pt] = s[:, pt * PRO_LANES:(pt + 1) * PRO_LANES]

        def top_body(idx, carry):
            hc = idx // npl
            pt = idx % npl
            ws = _sort_desc([s_ref[hc, pt, i * SUBLANES:(i + 1) * SUBLANES, :] for i in range(N_KEYS // SUBLANES)])
            rows = _top_rows(ws, N_TOP)
            sv_ref[hc, pt] = jnp.full((SV_ROWS, PRO_LANES), NEG_INF, F32)
            for k in range(N_TOP):
                sv_ref[hc, pt, k:k + 1, :] = rows[k]
            return carry

        lax.fori_loop(0, 2 * PEER_HEADS * npl, top_body, 0)

        def thr_body(idx, carry):
            h = idx // npl
            pt = idx % npl
            sv1 = sv_ref[2 * h, pt]
            sv2 = sv_ref[2 * h + 1, pt]
            pieces = [sv1[0:1, :] + sv2[i * SUBLANES:(i + 1) * SUBLANES, :] for i in range(SV_ROWS // SUBLANES)]
            pieces += [sv1[a:a + 1, :] + sv2[0:SUBLANES, :] for a in range(1, SUBLANES)]
            pieces += [sv1[i * SUBLANES:(i + 1) * SUBLANES, :] + sv2[0:1, :] for i in range(1, SV_ROWS // SUBLANES)]
            best = _top_rows(_sort_desc(pieces), N_TOP)
            zsum = jnp.ones_like(best[0])
            for k in range(1, PEER_TOPK):
                zsum = zsum + jnp.exp(best[k] - best[0])
            thr = 0.5 * (best[PEER_TOPK - 1] + best[PEER_TOPK])
            s1 = s_ref[2 * h, pt]
            s2 = s_ref[2 * h + 1, pt]
            a_ref[h, pt] = jnp.exp(s1 - sv1[0:1, :]) / zsum
            b_ref[h, pt] = thr - s1
            e2_ref[h, pt] = jnp.exp(s2 - sv2[0:1, :])
            return carry

        lax.fori_loop(0, PEER_HEADS * npl, thr_body, 0)
        acc_ref[...] = jnp.zeros_like(acc_ref)
        ct_ref[(ts - 1) * et:ts * et, tb - MXU_COLS:tb] = jnp.zeros((et, MXU_COLS), BF16)

    pieces = [(t, c) for t in range(ts) for c in range(tb // MXU_COLS)]
    units = [(t, c, rr) for (t, c) in pieces for rr in range(rows_per_tile)]
    out_rows = D_MODEL // rows_per_tile

    def pre_activations(t, c, rr):
        er = slice(t * et + rr * N_KEYS, t * et + (rr + 1) * N_KEYS)
        cols = slice(c * MXU_COLS, (c + 1) * MXU_COLS)
        ht_ref[er, cols] = jnp.dot(u_ref[er, :], xnt_ref[:, cols], preferred_element_type=F32)

    def output_term(t, c, m, v_ref=None):
        rows = slice(t * et, (t + 1) * et)
        cols = slice(c * MXU_COLS, (c + 1) * MXU_COLS)
        mr = slice(m * out_rows, (m + 1) * out_rows)
        v = vt_ref[t, mr, :] if v_ref is None else v_ref[0, mr, :]
        acc_ref[mr, cols] += jnp.dot(v, ct_ref[rows, cols], preferred_element_type=F32)

    def coefficient_tile(t, c, rr, l):
        i1 = (j * ts + t) * rows_per_tile + rr
        lt = c * (MXU_COLS // LANES) + l
        pt = lt // 2
        ln = slice((lt % 2) * LANES, (lt % 2 + 1) * LANES)
        g = jnp.zeros((N_KEYS, LANES), F32)
        for h in range(PEER_HEADS):
            a_row = a_ref[h, pt, pl.ds(i1, 1), :][:, ln]
            b_row = b_ref[h, pt, pl.ds(i1, 1), :][:, ln]
            g = g + jnp.where(s_ref[2 * h + 1, pt, :, ln] >= b_row, e2_ref[h, pt, :, ln], 0.0) * a_row
        er = slice(t * et + rr * N_KEYS, t * et + (rr + 1) * N_KEYS)
        tl = slice(lt * LANES, (lt + 1) * LANES)
        ct_ref[er, tl] = (g * _gelu_sigmoid_form(ht_ref[er, tl])).astype(BF16)

    pre_activations(*units[0])
    for n, (t, c, rr) in enumerate(units):
        if n + 1 < len(units):
            pre_activations(*units[n + 1])
        coefficient_tile(t, c, rr, 0)
        if n >= rows_per_tile:
            pt_, pc_, _ = units[n - rows_per_tile]
            output_term(pt_, pc_, rr)
        else:
            output_term(*pieces[-1], rr, v_ref=vtp_ref)
        coefficient_tile(t, c, rr, 1)

    @pl.when(j == pl.num_programs(1) - 1)
    def _():
        for m in range(rows_per_tile):
            output_term(*pieces[-1], m)
        y = x_ref[...] + acc_ref[...].T
        if final:
            y = _rms(y, fg_ref[...])
        o_ref[...] = y


def _peer(x, g, wqt, keys, u_tab, vt_tiles, final_g, final, tb, et, ts):
    n = x.shape[0]
    npl = tb // PRO_LANES
    return pl.pallas_call(
        functools.partial(_peer_kernel, tb, et, ts, final),
        grid=(n // tb, N_EXPERTS // (ts * et)),
        in_specs=[
            pl.BlockSpec((tb, D_MODEL), lambda i, j: (i, 0)),
            pl.BlockSpec((1, D_MODEL), lambda i, j: (0, 0)),
            pl.BlockSpec(wqt.shape, lambda i, j: (0, 0)),
            pl.BlockSpec(keys.shape, lambda i, j: (0, 0, 0)),
            pl.BlockSpec((ts * et, D_MODEL), lambda i, j: (j, 0)),
            pl.BlockSpec((ts, D_MODEL, et), lambda i, j: (j, 0, 0)),
            pl.BlockSpec((1, D_MODEL, et), lambda i, j: (jnp.maximum(j * ts - 1, 0), 0, 0)),
            pl.BlockSpec((1, D_MODEL), lambda i, j: (0, 0)),
        ],
        out_specs=pl.BlockSpec((tb, D_MODEL), lambda i, j: (i, 0)),
        out_shape=jax.ShapeDtypeStruct((n, D_MODEL), F32),
        scratch_shapes=[
            pltpu.VMEM((D_MODEL, tb), BF16),
            pltpu.VMEM((2 * PEER_HEADS * KEY_DIM, tb), BF16),
            pltpu.VMEM((2 * PEER_HEADS, npl, N_KEYS, PRO_LANES), F32),
            pltpu.VMEM((2 * PEER_HEADS, npl, SV_ROWS, PRO_LANES), F32),
            pltpu.VMEM((PEER_HEADS, npl, N_KEYS, PRO_LANES), F32),
            pltpu.VMEM((PEER_HEADS, npl, N_KEYS, PRO_LANES), F32),
            pltpu.VMEM((PEER_HEADS, npl, N_KEYS, PRO_LANES), F32),
            pltpu.VMEM((ts * et, tb), F32),
            pltpu.VMEM((ts * et, tb), BF16),
            pltpu.VMEM((D_MODEL, tb), F32),
        ],
        compiler_params=pltpu.CompilerParams(
            dimension_semantics=("arbitrary", "arbitrary"), vmem_limit_bytes=VMEM_LIMIT),
        name="peer_final" if final else "peer",
    )(x, g, wqt, keys, u_tab, vt_tiles, vt_tiles, final_g)


PROMPT_NB = 8
PROMPT_TT = 32
SAMPLE_NB = 32
PEER_TB = 512
PEER_ET = 512
PEER_TS = 4


def _to_chunks(a, nb):
    b, t, c = a.shape
    return a.reshape(b // nb, nb, t, c).transpose(0, 2, 1, 3).reshape(b * t, c)


def _from_chunks(a, nb, b, t):
    c = a.shape[-1]
    return a.reshape(b // nb, t, nb, c).transpose(0, 2, 1, 3).reshape(b, t, c)


def kernel(x_prompt, x_sample, state_conv_a, state_pool, state_conv_c, state_lru_h, state_conv_d, norm1_g, w_in, b_gate, conv_a_w, pool_w, pool_scale, conv_c_w, conv_c_b, lru_wa, lru_ba, lru_wi, lru_bi, lru_lambda, conv_d_w, conv_d_b, ln_d_g, ln_d_b, w_pa, w_pb, w_pc, w_pd, w_out, norm2_g, peer_wq, peer_keys, peer_u, peer_v, final_g):
    bp, tp, _ = x_prompt.shape
    bs, ts, _ = x_sample.shape
    xp = _to_chunks(x_prompt, PROMPT_NB)
    xs = _to_chunks(x_sample, SAMPLE_NB)
    fg = final_g.reshape(1, D_MODEL)

    outs_p = [[] for _ in range(5)]
    outs_s = [[] for _ in range(5)]
    for l in range(DEPTH):
        lp = {
            'conv_a_w': conv_a_w[l], 'pool_w': pool_w[l].astype(BF16), 'pool_scale': pool_scale[l].reshape(1, W_B),
            'conv_c_w': conv_c_w[l], 'conv_c_b': conv_c_b[l].reshape(1, W_C),
            'lru_wa': lru_wa[l].astype(BF16), 'lru_ba': lru_ba[l].reshape(1, W_C),
            'lru_wi': lru_wi[l].astype(BF16), 'lru_bi': lru_bi[l].reshape(1, W_C),
            'lru_lambda': lru_lambda[l].reshape(1, W_C),
            'conv_d_w': conv_d_w[l], 'conv_d_b': conv_d_b[l].reshape(1, W_D),
            'ln_d_g': ln_d_g[l].reshape(1, W_D), 'ln_d_b': ln_d_b[l].reshape(1, W_D),
            'b_gate': b_gate[l].reshape(1, 4 * D_MODEL),
            'w_pa': w_pa[l].astype(BF16), 'w_pb': w_pb[l].astype(BF16), 'w_pc': w_pc[l].astype(BF16),
            'w_pd': w_pd[l].astype(BF16), 'w_out': w_out[l].astype(BF16),
        }
        g1 = norm1_g[l].reshape(1, D_MODEL)
        g2 = norm2_g[l].reshape(1, D_MODEL)
        w_in_b = w_in[l].astype(BF16)
        wqt = peer_wq[l].T.astype(BF16)
        keys = peer_keys[l].reshape(2 * PEER_HEADS, N_KEYS, KEY_DIM).astype(BF16)
        u_tab = peer_u[l].astype(BF16)
        vt_tab = peer_v[l].astype(BF16).reshape(N_EXPERTS // PEER_ET, PEER_ET, D_MODEL).transpose(0, 2, 1)
        final = l == DEPTH - 1

        zero_hist = (jnp.zeros(((CONV_A - 1) * bp, W_A), F32), jnp.zeros((POOL_BUF * bp, W_B), F32),
                     jnp.zeros(((CONV_C - 1) * bp, W_C), F32), jnp.zeros((bp, W_C), F32),
                     jnp.zeros(((CONV_D - 1) * bp, W_D), F32))
        u = _inproj(xp, g1, w_in_b, 1024, 1536)
        xp, *st = _mixer(u, xp, zero_hist, lp, PROMPT_NB, PROMPT_TT, True, 0)
        for acc, s in zip(outs_p, st):
            acc.append(s)
        xp = _peer(xp, g2, wqt, keys, u_tab, vt_tab, fg, final, PEER_TB, PEER_ET, PEER_TS)

        hist = (_to_chunks(state_conv_a[l], SAMPLE_NB), _to_chunks(state_pool[l], SAMPLE_NB),
                _to_chunks(state_conv_c[l], SAMPLE_NB), state_lru_h[l], _to_chunks(state_conv_d[l], SAMPLE_NB))
        u = _inproj(xs, g1, w_in_b, 512, 1536)
        xs, *st = _mixer(u, xs, hist, lp, SAMPLE_NB, ts, False, PAST_LEN)
        for acc, s in zip(outs_s, st):
            acc.append(s)
        xs = _peer(xs, g2, wqt, keys, u_tab, vt_tab, fg, final, PEER_TB, PEER_ET, PEER_TS)

    y_prompt = _from_chunks(xp, PROMPT_NB, bp, tp)
    y_sample = _from_chunks(xs, SAMPLE_NB, bs, ts)

    def states(outs, nb, b):
        res = []
        for idx, hist_len in ((0, CONV_A - 1), (1, POOL_BUF), (2, CONV_C - 1), (3, None), (4, CONV_D - 1)):
            if hist_len is None:
                res.append(jnp.stack(outs[idx]))
            else:
                res.append(jnp.stack([_from_chunks(s, nb, b, hist_len) for s in outs[idx]]))
        return res

    ca_p, pool_p, cc_p, h_p, cd_p = states(outs_p, PROMPT_NB, bp)
    ca_s, pool_s, cc_s, h_s, cd_s = states(outs_s, SAMPLE_NB, bs)
    return (y_prompt, y_sample, ca_p, ca_s, pool_p, pool_s, cc_p, cc_s, h_p, h_s, cd_p, cd_s)
```

```python
import functools

import jax
import jax.numpy as jnp
from jax import lax
from jax.experimental import pallas as pl
from jax.experimental.pallas import tpu as pltpu

F32 = jnp.float32
BF16 = jnp.bfloat16

D_MODEL = 1024
DEPTH = 2
PAST_LEN = 16384
W_A = 512
CONV_A = 3
W_B = 512
POOL_WINDOWS = (2, 4, 8, 16)
POOL_GROUP = 128
POOL_BUF = 15
W_C = 1024
LRU_HEADS = 8
LRU_HEAD_DIM = 128
CONV_C = 4
LRU_C = 8.0
W_D = 512
CONV_D = 31
OFF_B = 3 * W_A
OFF_C = OFF_B + W_B
OFF_D = OFF_C + 2 * W_C
OFF_G = OFF_D + 2 * W_D
IN_TOTAL = OFF_G + 4 * D_MODEL
PEER_HEADS = 8
N_KEYS = 128
N_EXPERTS = N_KEYS * N_KEYS
KEY_DIM = 128
PEER_TOPK = 16
EPS = 1e-6

LANES = 128
VMEM_LIMIT = 56 * 1024 * 1024

GELU_C0 = 0.7978845608028654
GELU_C1 = 0.044715
LOG2_E = 1.4426950408889634
NEG_INF = float("-inf")


def _sigmoid(x):
    return 1.0 / (1.0 + jnp.exp(-x))


def _gelu(x):
    return 0.5 * x * (1.0 + jnp.tanh(GELU_C0 * (x + GELU_C1 * (x * x * x))))


def _gelu_sigmoid_form(x):
    q = x * (x * x * (-2.0 * GELU_C0 * GELU_C1 * LOG2_E) + (-2.0 * GELU_C0 * LOG2_E))
    return x / (1.0 + jnp.exp2(q))


def _rms(x, g):
    ms = jnp.mean(x * x, axis=-1, keepdims=True)
    return x * lax.rsqrt(ms + EPS) * g


def _bdot(a, b):
    return jnp.dot(a.astype(BF16), b.astype(BF16), preferred_element_type=F32)


def _inproj_kernel(x_ref, g_ref, w_ref, u_ref, xn_ref):
    @pl.when(pl.program_id(1) == 0)
    def _():
        xn_ref[...] = _rms(x_ref[...], g_ref[...]).astype(BF16)

    u_ref[...] = jnp.dot(xn_ref[...], w_ref[...], preferred_element_type=F32)


def _inproj(x, g, w_bf16, tm, tn):
    n = x.shape[0]
    return pl.pallas_call(
        _inproj_kernel,
        grid=(n // tm, IN_TOTAL // tn),
        in_specs=[
            pl.BlockSpec((tm, D_MODEL), lambda i, j: (i, 0)),
            pl.BlockSpec((1, D_MODEL), lambda i, j: (0, 0)),
            pl.BlockSpec((D_MODEL, tn), lambda i, j: (0, j)),
        ],
        out_specs=pl.BlockSpec((tm, tn), lambda i, j: (i, j)),
        out_shape=jax.ShapeDtypeStruct((n, IN_TOTAL), F32),
        scratch_shapes=[pltpu.VMEM((tm, D_MODEL), BF16)],
        compiler_params=pltpu.CompilerParams(
            dimension_semantics=("arbitrary", "arbitrary"), vmem_limit_bytes=VMEM_LIMIT),
        name="inproj",
    )(x, g, w_bf16)


def _mixer_kernel(nb, tt, carry, offset,
                  u_ref, x_ref, ha_ref, hb_ref, hc_ref, h0_ref, hd_ref,
                  caw_ref, pw_ref, ps_ref, ccw_ref, ccb_ref, wa_ref, ba_ref, wi_ref, bi_ref, lam_ref,
                  cdw_ref, cdb_ref, lng_ref, lnb_ref, bg_ref,
                  wpa_ref, wpb_ref, wpc_ref, wpd_ref, wout_ref,
                  xo_ref, sa_ref, sb_ref, sc_ref, hl_ref, sd_ref,
                  ea_ref, eb_ref, ec_ref, ed_ref, h_ref, a_ref, bx_ref, hs_ref, yb_ref, gate_ref):
    r = tt * nb
    step = pl.program_id(0)
    first = (step == 0) if carry else None

    def load_hist():
        ea_ref[0:(CONV_A - 1) * nb, :] = ha_ref[...]
        eb_ref[0:POOL_BUF * nb, :] = hb_ref[...]
        ec_ref[0:(CONV_C - 1) * nb, :] = hc_ref[...]
        ed_ref[0:(CONV_D - 1) * nb, :] = hd_ref[...]
        h_ref[...] = h0_ref[...]

    if carry:
        pl.when(first)(load_hist)
    else:
        load_hist()

    z = u_ref[:, W_A:2 * W_A] * u_ref[:, 2 * W_A:3 * W_A]
    ea_ref[(CONV_A - 1) * nb:, :] = z
    y = caw_ref[0:1, :] * ea_ref[0:r, :]
    for k in range(1, CONV_A):
        y = y + caw_ref[k:k + 1, :] * ea_ref[k * nb:k * nb + r, :]
    ya = u_ref[:, 0:W_A] * y

    eb_ref[POOL_BUF * nb:, :] = u_ref[:, OFF_B:OFF_C]
    row = lax.broadcasted_iota(jnp.int32, (r, POOL_GROUP), 0)
    tpos = (row >> (nb.bit_length() - 1)) + (step * tt if carry else 0) + offset
    for g, w in enumerate(POOL_WINDOWS):
        sl = slice(g * POOL_GROUP, (g + 1) * POOL_GROUP)
        acc = eb_ref[POOL_BUF * nb:POOL_BUF * nb + r, sl]
        for k in range(1, w):
            acc = acc + eb_ref[(POOL_BUF - k) * nb:(POOL_BUF - k) * nb + r, sl]
        cnt = jnp.minimum(tpos + 1, w).astype(F32)
        p = acc / cnt - eb_ref[POOL_BUF * nb:POOL_BUF * nb + r, sl]
        yb_ref[:, sl] = _bdot(p, pw_ref[g]) * ps_ref[:, sl]

    ec_ref[(CONV_C - 1) * nb:, :] = u_ref[:, OFF_C:OFF_C + W_C]
    xconv = ccw_ref[0:1, :] * ec_ref[0:r, :]
    for k in range(1, CONV_C):
        xconv = xconv + ccw_ref[k:k + 1, :] * ec_ref[k * nb:k * nb + r, :]
    xconv = xconv + ccb_ref[...]
    lam = lam_ref[...]
    nsp = -LRU_C * (jnp.maximum(-lam, 0.0) + jnp.log(1.0 + jnp.exp(-jnp.abs(lam))))
    for hh in range(LRU_HEADS):
        sl = slice(hh * LRU_HEAD_DIM, (hh + 1) * LRU_HEAD_DIM)
        xh = xconv[:, sl]
        xhb = xh.astype(BF16)
        rg = _sigmoid(jnp.dot(xhb, wa_ref[hh], preferred_element_type=F32) + ba_ref[:, sl])
        ig = _sigmoid(jnp.dot(xhb, wi_ref[hh], preferred_element_type=F32) + bi_ref[:, sl])
        log_a = rg * nsp[:, sl]
        a = jnp.exp(log_a)
        a_ref[:, sl] = a
        bx_ref[:, sl] = jnp.sqrt(1.0 - a * a) * (ig * xh)

    if tt <= 8:
        for t in range(tt):
            h = a_ref[t * nb:(t + 1) * nb, :] * h_ref[...] + bx_ref[t * nb:(t + 1) * nb, :]
            h_ref[...] = h
            hs_ref[t * nb:(t + 1) * nb, :] = h
    else:
        def scan_body(t, h):
            rows = pl.ds(pl.multiple_of(t * nb, nb), nb)
            h = a_ref[rows, :] * h + bx_ref[rows, :]
            hs_ref[rows, :] = h
            return h
        h_ref[...] = lax.fori_loop(0, tt, scan_body, h_ref[...])
    yc = hs_ref[...] * _gelu(u_ref[:, OFF_C + W_C:OFF_D])

    ed_ref[(CONV_D - 1) * nb:, :] = u_ref[:, OFF_D:OFF_D + W_D] * _sigmoid(u_ref[:, OFF_D + W_D:OFF_G])
    c = cdw_ref[0:1, :] * ed_ref[0:r, :]
    for k in range(1, CONV_D):
        c = c + cdw_ref[k:k + 1, :] * ed_ref[k * nb:k * nb + r, :]
    c = c + cdb_ref[...]
    mu = jnp.mean(c, axis=-1, keepdims=True)
    cc = c - mu
    var = jnp.mean(cc * cc, axis=-1, keepdims=True)
    ln = cc * lax.rsqrt(var + EPS) * lng_ref[...] + lnb_ref[...]
    yd = ln * _sigmoid(ln)

    gate_ref[...] = _sigmoid(u_ref[:, OFF_G:] + bg_ref[...])
    m = gate_ref[:, 0:D_MODEL] * _bdot(ya, wpa_ref[...])
    m = m + gate_ref[:, D_MODEL:2 * D_MODEL] * _bdot(yb_ref[...], wpb_ref[...])
    m = m + gate_ref[:, 2 * D_MODEL:3 * D_MODEL] * _bdot(yc, wpc_ref[...])
    m = m + gate_ref[:, 3 * D_MODEL:] * _bdot(yd, wpd_ref[...])
    xo_ref[...] = x_ref[...] + _bdot(m, wout_ref[...])

    def store_state():
        sa_ref[...] = ea_ref[r:r + (CONV_A - 1) * nb, :]
        sb_ref[...] = eb_ref[r:r + POOL_BUF * nb, :]
        sc_ref[...] = ec_ref[r:r + (CONV_C - 1) * nb, :]
        sd_ref[...] = ed_ref[r:r + (CONV_D - 1) * nb, :]
        hl_ref[...] = h_ref[...]

    if carry:
        pl.when(step == pl.num_programs(0) - 1)(store_state)
        for e_ref, hist in ((ea_ref, CONV_A - 1), (eb_ref, POOL_BUF), (ec_ref, CONV_C - 1), (ed_ref, CONV_D - 1)):
            total = hist * nb
            for s in range(0, total, r):
                n = min(r, total - s)
                e_ref[s:s + n, :] = e_ref[r + s:r + s + n, :]
    else:
        store_state()


def _mixer(u, x, hists, lp, nb, tt, carry, offset):
    n = x.shape[0]
    r = nb * tt
    steps = n // r
    ha, hb, hc, h0, hd = hists
    widths = (W_A, W_B, W_C, W_D)
    hist_rows = ((CONV_A - 1) * nb, POOL_BUF * nb, (CONV_C - 1) * nb, (CONV_D - 1) * nb)

    def chunk_map(i):
        return (0, 0) if carry else (i, 0)

    def const2(i):
        return (0, 0)

    def const3(i):
        return (0, 0, 0)

    def full(a):
        return pl.BlockSpec(a.shape, const2 if a.ndim == 2 else const3)

    params = [lp['conv_a_w'], lp['pool_w'], lp['pool_scale'], lp['conv_c_w'], lp['conv_c_b'],
              lp['lru_wa'], lp['lru_ba'], lp['lru_wi'], lp['lru_bi'], lp['lru_lambda'],
              lp['conv_d_w'], lp['conv_d_b'], lp['ln_d_g'], lp['ln_d_b'], lp['b_gate'],
              lp['w_pa'], lp['w_pb'], lp['w_pc'], lp['w_pd'], lp['w_out']]
    in_specs = [
        pl.BlockSpec((r, IN_TOTAL), lambda i: (i, 0)),
        pl.BlockSpec((r, D_MODEL), lambda i: (i, 0)),
        pl.BlockSpec((hist_rows[0], W_A), chunk_map),
        pl.BlockSpec((hist_rows[1], W_B), chunk_map),
        pl.BlockSpec((hist_rows[2], W_C), chunk_map),
        pl.BlockSpec((nb, W_C), chunk_map),
        pl.BlockSpec((hist_rows[3], W_D), chunk_map),
    ] + [full(p) for p in params]
    n_state = 1 if carry else steps
    out_shape = (
        jax.ShapeDtypeStruct((n, D_MODEL), F32),
        jax.ShapeDtypeStruct((n_state * hist_rows[0], W_A), F32),
        jax.ShapeDtypeStruct((n_state * hist_rows[1], W_B), F32),
        jax.ShapeDtypeStruct((n_state * hist_rows[2], W_C), F32),
        jax.ShapeDtypeStruct((n_state * nb, W_C), F32),
        jax.ShapeDtypeStruct((n_state * hist_rows[3], W_D), F32),
    )
    out_specs = (
        pl.BlockSpec((r, D_MODEL), lambda i: (i, 0)),
        pl.BlockSpec((hist_rows[0], W_A), chunk_map),
        pl.BlockSpec((hist_rows[1], W_B), chunk_map),
        pl.BlockSpec((hist_rows[2], W_C), chunk_map),
        pl.BlockSpec((nb, W_C), chunk_map),
        pl.BlockSpec((hist_rows[3], W_D), chunk_map),
    )
    scratch = [pltpu.VMEM((hr + r, w), F32) for hr, w in zip(hist_rows, widths)]
    scratch += [
        pltpu.VMEM((nb, W_C), F32),
        pltpu.VMEM((r, W_C), F32),
        pltpu.VMEM((r, W_C), F32),
        pltpu.VMEM((r, W_C), F32),
        pltpu.VMEM((r, W_B), F32),
        pltpu.VMEM((r, 4 * D_MODEL), F32),
    ]
    return pl.pallas_call(
        functools.partial(_mixer_kernel, nb, tt, carry, offset),
        grid=(steps,),
        in_specs=in_specs,
        out_specs=out_specs,
        out_shape=out_shape,
        scratch_shapes=scratch,
        compiler_params=pltpu.CompilerParams(
            dimension_semantics=("arbitrary",), vmem_limit_bytes=VMEM_LIMIT),
        name="mixer_carry" if carry else "mixer_batch",
    )(u, x, ha, hb, hc, h0, hd, *params)


N_TOP = PEER_TOPK + 1
SV_ROWS = 24


PRO_LANES = 2 * LANES
SUBLANES = 8
MXU_COLS = 256


def _batcher_pairs(n):
    pairs = []

    def merge(lo, hi, r):
        step = r * 2
        if step < hi - lo:
            merge(lo, hi, step)
            merge(lo + r, hi, step)
            pairs.extend((i, i + r) for i in range(lo + r, hi - r, step))
        else:
            pairs.append((lo, lo + r))

    def sort(lo, hi):
        if hi - lo >= 1:
            mid = lo + (hi - lo) // 2
            sort(lo, mid)
            sort(mid + 1, hi)
            merge(lo, hi, 1)

    sort(0, n - 1)
    return pairs


def _sort_desc(ws):
    n = 1
    while n < len(ws):
        n *= 2
    ws = list(ws) + [None] * (n - len(ws))
    for i, j in _batcher_pairs(n):
        a, b = ws[i], ws[j]
        if b is None:
            continue
        if a is None:
            ws[i], ws[j] = b, None
        else:
            ws[i], ws[j] = jnp.maximum(a, b), jnp.minimum(a, b)
    return [w for w in ws if w is not None]


def _top_rows(ws, n):
    out = []
    for k in range(n):
        m = jnp.max(ws[0], axis=0, keepdims=True)
        out.append(m)
        remaining = n - 1 - k
        if remaining == 0:
            break
        hit = ws[0] >= m
        ws = [jnp.where(hit, ws[i + 1] if i + 1 < len(ws) else NEG_INF, ws[i])
              for i in range(min(len(ws), remaining))]
    return out


def _peer_kernel(tb, et, ts, final,
                 x_ref, g_ref, wqt_ref, keys_ref, u_ref, vt_ref, vtp_ref, fg_ref,
                 o_ref,
                 xnt_ref, qt_ref, s_ref, sv_ref, a_ref, b_ref, e2_ref, ht_ref, ct_ref, acc_ref):
    npl = tb // PRO_LANES
    rows_per_tile = et // N_KEYS
    j = pl.program_id(1)

    @pl.when(j == 0)
    def _():
        xn = _rms(x_ref[...], g_ref[...])
        xnt_ref[...] = xn.T.astype(BF16)
        qt_ref[...] = jnp.dot(wqt_ref[...], xnt_ref[...], preferred_element_type=F32).astype(BF16)
        for hc in range(2 * PEER_HEADS):
            s = jnp.dot(keys_ref[hc], qt_ref[hc * KEY_DIM:(hc + 1) * KEY_DIM, :],
                        preferred_element_type=F32)
            for pt in range(npl):
                s_ref[hc, pt] = s[:, pt * PRO_LANES:(pt + 1) * PRO_LANES]

        def select_body(idx, carry):
            h = idx // npl
            pt = idx % npl
            for hc in (2 * h, 2 * h + 1):
                ws = _sort_desc([s_ref[hc, pt, i * SUBLANES:(i + 1) * SUBLANES, :]
                                 for i in range(N_KEYS // SUBLANES)])
                rows = _top_rows(ws, N_TOP)
                sv_ref[hc, pt] = jnp.full((SV_ROWS, PRO_LANES), NEG_INF, F32)
                for k in range(N_TOP):
                    sv_ref[hc, pt, k:k + 1, :] = rows[k]
            sv1 = sv_ref[2 * h, pt]
            sv2 = sv_ref[2 * h + 1, pt]
            pieces = [sv1[0:1, :] + sv2[i * SUBLANES:(i + 1) * SUBLANES, :] for i in range(SV_ROWS // SUBLANES)]
            pieces += [sv1[a:a + 1, :] + sv2[0:SUBLANES, :] for a in range(1, SUBLANES)]
            pieces += [sv1[i * SUBLANES:(i + 1) * SUBLANES, :] + sv2[0:1, :] for i in range(1, SV_ROWS // SUBLANES)]
            best = _top_rows(_sort_desc(pieces), N_TOP)
            zsum = jnp.ones_like(best[0])
            for k in range(1, PEER_TOPK):
                zsum = zsum + jnp.exp(best[k] - best[0])
            thr = 0.5 * (best[PEER_TOPK - 1] + best[PEER_TOPK])
            s1 = s_ref[2 * h, pt]
            s2 = s_ref[2 * h + 1, pt]
            a_ref[h, pt] = jnp.exp(s1 - sv1[0:1, :]) / zsum
            b_ref[h, pt] = thr - s1
            e2_ref[h, pt] = jnp.exp(s2 - sv2[0:1, :])
            return carry

        lax.fori_loop(0, PEER_HEADS * npl, select_body, 0)
        acc_ref[...] = jnp.zeros_like(acc_ref)
        ct_ref[(ts - 1) * et:ts * et, tb - MXU_COLS:tb] = jnp.zeros((et, MXU_COLS), BF16)

    pieces = [(t, c) for t in range(ts) for c in range(tb // MXU_COLS)]
    units = [(t, c, rr) for (t, c) in pieces for rr in range(rows_per_tile)]
    out_rows = D_MODEL // rows_per_tile

    def pre_activations(t, c, rr):
        er = slice(t * et + rr * N_KEYS, t * et + (rr + 1) * N_KEYS)
        cols = slice(c * MXU_COLS, (c + 1) * MXU_COLS)
        ht_ref[er, cols] = jnp.dot(u_ref[er, :], xnt_ref[:, cols], preferred_element_type=F32)

    def output_term(t, c, m, v_ref=None):
        rows = slice(t * et, (t + 1) * et)
        cols = slice(c * MXU_COLS, (c + 1) * MXU_COLS)
        mr = slice(m * out_rows, (m + 1) * out_rows)
        v = vt_ref[t, mr, :] if v_ref is None else v_ref[0, mr, :]
        acc_ref[mr, cols] += jnp.dot(v, ct_ref[rows, cols], preferred_element_type=F32)

    def coefficient_tile(t, c, rr, l):
        i1 = (j * ts + t) * rows_per_tile + rr
        lt = c * (MXU_COLS // LANES) + l
        pt = lt // 2
        ln = slice((lt % 2) * LANES, (lt % 2 + 1) * LANES)
        g = jnp.zeros((N_KEYS, LANES), F32)
        for h in range(PEER_HEADS):
            a_row = a_ref[h, pt, pl.ds(i1, 1), :][:, ln]
            b_row = b_ref[h, pt, pl.ds(i1, 1), :][:, ln]
            g = g + jnp.where(s_ref[2 * h + 1, pt, :, ln] >= b_row, e2_ref[h, pt, :, ln], 0.0) * a_row
        er = slice(t * et + rr * N_KEYS, t * et + (rr + 1) * N_KEYS)
        tl = slice(lt * LANES, (lt + 1) * LANES)
        ct_ref[er, tl] = (g * _gelu_sigmoid_form(ht_ref[er, tl])).astype(BF16)

    pre_activations(*units[0])
    for n, (t, c, rr) in enumerate(units):
        if n + 1 < len(units):
            pre_activations(*units[n + 1])
        coefficient_tile(t, c, rr, 0)
        if n >= rows_per_tile:
            pt_, pc_, _ = units[n - rows_per_tile]
            output_term(pt_, pc_, rr)
        else:
            output_term(*pieces[-1], rr, v_ref=vtp_ref)
        coefficient_tile(t, c, rr, 1)

    @pl.when(j == pl.num_programs(1) - 1)
    def _():
        for m in range(rows_per_tile):
            output_term(*pieces[-1], m)
        y = x_ref[...] + acc_ref[...].T
        if final:
            y = _rms(y, fg_ref[...])
        o_ref[...] = y


def _peer(x, g, wqt, keys, u_tab, vt_tiles, final_g, final, tb, et, ts):
    n = x.shape[0]
    npl = tb // PRO_LANES
    return pl.pallas_call(
        functools.partial(_peer_kernel, tb, et, ts, final),
        grid=(n // tb, N_EXPERTS // (ts * et)),
        in_specs=[
            pl.BlockSpec((tb, D_MODEL), lambda i, j: (i, 0)),
            pl.BlockSpec((1, D_MODEL), lambda i, j: (0, 0)),
            pl.BlockSpec(wqt.shape, lambda i, j: (0, 0)),
            pl.BlockSpec(keys.shape, lambda i, j: (0, 0, 0)),
            pl.BlockSpec((ts * et, D_MODEL), lambda i, j: (j, 0)),
            pl.BlockSpec((ts, D_MODEL, et), lambda i, j: (j, 0, 0)),
            pl.BlockSpec((1, D_MODEL, et), lambda i, j: (jnp.maximum(j * ts - 1, 0), 0, 0)),
            pl.BlockSpec((1, D_MODEL), lambda i, j: (0, 0)),
        ],
        out_specs=pl.BlockSpec((tb, D_MODEL), lambda i, j: (i, 0)),
        out_shape=jax.ShapeDtypeStruct((n, D_MODEL), F32),
        scratch_shapes=[
            pltpu.VMEM((D_MODEL, tb), BF16),
            pltpu.VMEM((2 * PEER_HEADS * KEY_DIM, tb), BF16),
            pltpu.VMEM((2 * PEER_HEADS, npl, N_KEYS, PRO_LANES), F32),
            pltpu.VMEM((2 * PEER_HEADS, npl, SV_ROWS, PRO_LANES), F32),
            pltpu.VMEM((PEER_HEADS, npl, N_KEYS, PRO_LANES), F32),
            pltpu.VMEM((PEER_HEADS, npl, N_KEYS, PRO_LANES), F32),
            pltpu.VMEM((PEER_HEADS, npl, N_KEYS, PRO_LANES), F32),
            pltpu.VMEM((ts * et, tb), F32),
            pltpu.VMEM((ts * et, tb), BF16),
            pltpu.VMEM((D_MODEL, tb), F32),
        ],
        compiler_params=pltpu.CompilerParams(
            dimension_semantics=("arbitrary", "arbitrary"), vmem_limit_bytes=VMEM_LIMIT),
        name="peer_final" if final else "peer",
    )(x, g, wqt, keys, u_tab, vt_tiles, vt_tiles, final_g)


PROMPT_NB = 8
PROMPT_TT = 32
SAMPLE_NB = 32
INPROJ_TM = 1024
INPROJ_TN = 1536
PEER_TB = 512
PEER_ET = 512
PEER_TS = 4


def _to_chunks(a, nb):
    b, t, c = a.shape
    return a.reshape(b // nb, nb, t, c).transpose(0, 2, 1, 3).reshape(b * t, c)


def _from_chunks(a, nb, b, t):
    c = a.shape[-1]
    return a.reshape(b // nb, t, nb, c).transpose(0, 2, 1, 3).reshape(b, t, c)


def kernel(x_prompt, x_sample, state_conv_a, state_pool, state_conv_c, state_lru_h, state_conv_d, norm1_g, w_in, b_gate, conv_a_w, pool_w, pool_scale, conv_c_w, conv_c_b, lru_wa, lru_ba, lru_wi, lru_bi, lru_lambda, conv_d_w, conv_d_b, ln_d_g, ln_d_b, w_pa, w_pb, w_pc, w_pd, w_out, norm2_g, peer_wq, peer_keys, peer_u, peer_v, final_g):
    bp, tp, _ = x_prompt.shape
    bs, ts, _ = x_sample.shape
    xp = _to_chunks(x_prompt, PROMPT_NB)
    xs = _to_chunks(x_sample, SAMPLE_NB)
    fg = final_g.reshape(1, D_MODEL)

    outs_p = [[] for _ in range(5)]
    outs_s = [[] for _ in range(5)]
    for l in range(DEPTH):
        lp = {
            'conv_a_w': conv_a_w[l], 'pool_w': pool_w[l].astype(BF16), 'pool_scale': pool_scale[l].reshape(1, W_B),
            'conv_c_w': conv_c_w[l], 'conv_c_b': conv_c_b[l].reshape(1, W_C),
            'lru_wa': lru_wa[l].astype(BF16), 'lru_ba': lru_ba[l].reshape(1, W_C),
            'lru_wi': lru_wi[l].astype(BF16), 'lru_bi': lru_bi[l].reshape(1, W_C),
            'lru_lambda': lru_lambda[l].reshape(1, W_C),
            'conv_d_w': conv_d_w[l], 'conv_d_b': conv_d_b[l].reshape(1, W_D),
            'ln_d_g': ln_d_g[l].reshape(1, W_D), 'ln_d_b': ln_d_b[l].reshape(1, W_D),
            'b_gate': b_gate[l].reshape(1, 4 * D_MODEL),
            'w_pa': w_pa[l].astype(BF16), 'w_pb': w_pb[l].astype(BF16), 'w_pc': w_pc[l].astype(BF16),
            'w_pd': w_pd[l].astype(BF16), 'w_out': w_out[l].astype(BF16),
        }
        g1 = norm1_g[l].reshape(1, D_MODEL)
        g2 = norm2_g[l].reshape(1, D_MODEL)
        w_in_b = w_in[l].astype(BF16)
        wqt = peer_wq[l].T.astype(BF16)
        keys = peer_keys[l].reshape(2 * PEER_HEADS, N_KEYS, KEY_DIM).astype(BF16)
        u_tab = peer_u[l].astype(BF16)
        vt_tab = peer_v[l].astype(BF16).reshape(N_EXPERTS // PEER_ET, PEER_ET, D_MODEL).transpose(0, 2, 1)
        final = l == DEPTH - 1

        zero_hist = (jnp.zeros(((CONV_A - 1) * bp, W_A), F32), jnp.zeros((POOL_BUF * bp, W_B), F32),
                     jnp.zeros(((CONV_C - 1) * bp, W_C), F32), jnp.zeros((bp, W_C), F32),
                     jnp.zeros(((CONV_D - 1) * bp, W_D), F32))
        u = _inproj(xp, g1, w_in_b, INPROJ_TM, INPROJ_TN)
        xp, *st = _mixer(u, xp, zero_hist, lp, PROMPT_NB, PROMPT_TT, True, 0)
        for acc, s in zip(outs_p, st):
            acc.append(s)
        xp = _peer(xp, g2, wqt, keys, u_tab, vt_tab, fg, final, PEER_TB, PEER_ET, PEER_TS)

        hist = (_to_chunks(state_conv_a[l], SAMPLE_NB), _to_chunks(state_pool[l], SAMPLE_NB),
                _to_chunks(state_conv_c[l], SAMPLE_NB), state_lru_h[l], _to_chunks(state_conv_d[l], SAMPLE_NB))
        u = _inproj(xs, g1, w_in_b, min(INPROJ_TM, xs.shape[0]), INPROJ_TN)
        xs, *st = _mixer(u, xs, hist, lp, SAMPLE_NB, ts, False, PAST_LEN)
        for acc, s in zip(outs_s, st):
            acc.append(s)
        xs = _peer(xs, g2, wqt, keys, u_tab, vt_tab, fg, final, PEER_TB, PEER_ET, PEER_TS)

    y_prompt = _from_chunks(xp, PROMPT_NB, bp, tp)
    y_sample = _from_chunks(xs, SAMPLE_NB, bs, ts)

    def states(outs, nb, b):
        res = []
        for idx, hist_len in ((0, CONV_A - 1), (1, POOL_BUF), (2, CONV_C - 1), (3, None), (4, CONV_D - 1)):
            if hist_len is None:
                res.append(jnp.stack(outs[idx]))
            else:
                res.append(jnp.stack([_from_chunks(s, nb, b, hist_len) for s in outs[idx]]))
        return res

    ca_p, pool_p, cc_p, h_p, cd_p = states(outs_p, PROMPT_NB, bp)
    ca_s, pool_s, cc_s, h_s, cd_s = states(outs_s, SAMPLE_NB, bs)
    return (y_prompt, y_sample, ca_p, ca_s, pool_p, pool_s, cc_p, cc_s, h_p, h_s, cd_p, cd_s)
```

```python
import functools

import jax
import jax.numpy as jnp
from jax import lax
from jax.experimental import pallas as pl
from jax.experimental.pallas import tpu as pltpu

F32 = jnp.float32
BF16 = jnp.bfloat16

D_MODEL = 1024
DEPTH = 2
PAST_LEN = 16384
W_A = 512
CONV_A = 3
W_B = 512
POOL_WINDOWS = (2, 4, 8, 16)
POOL_GROUP = 128
POOL_BUF = 15
W_C = 1024
LRU_HEADS = 8
LRU_HEAD_DIM = 128
CONV_C = 4
LRU_C = 8.0
W_D = 512
CONV_D = 31
OFF_B = 3 * W_A
OFF_C = OFF_B + W_B
OFF_D = OFF_C + 2 * W_C
OFF_G = OFF_D + 2 * W_D
IN_TOTAL = OFF_G + 4 * D_MODEL
PEER_HEADS = 8
N_KEYS = 128
N_EXPERTS = N_KEYS * N_KEYS
KEY_DIM = 128
PEER_TOPK = 16
EPS = 1e-6

LANES = 128
VMEM_LIMIT = 56 * 1024 * 1024

GELU_C0 = 0.7978845608028654
GELU_C1 = 0.044715
LOG2_E = 1.4426950408889634
NEG_INF = float("-inf")


def _sigmoid(x):
    return 1.0 / (1.0 + jnp.exp(-x))


def _gelu(x):
    return 0.5 * x * (1.0 + jnp.tanh(GELU_C0 * (x + GELU_C1 * (x * x * x))))


def _gelu_sigmoid_form(x):
    q = x * (x * x * (-2.0 * GELU_C0 * GELU_C1 * LOG2_E) + (-2.0 * GELU_C0 * LOG2_E))
    return x / (1.0 + jnp.exp2(q))


def _rms(x, g):
    ms = jnp.mean(x * x, axis=-1, keepdims=True)
    return x * lax.rsqrt(ms + EPS) * g


def _bdot(a, b):
    return jnp.dot(a.astype(BF16), b.astype(BF16), preferred_element_type=F32)


def _inproj_kernel(x_ref, g_ref, w_ref, u_ref, xn_ref):
    @pl.when(pl.program_id(1) == 0)
    def _():
        xn_ref[...] = _rms(x_ref[...], g_ref[...]).astype(BF16)

    u_ref[...] = jnp.dot(xn_ref[...], w_ref[...], preferred_element_type=F32)


def _inproj(x, g, w_bf16, tm, tn):
    n = x.shape[0]
    return pl.pallas_call(
        _inproj_kernel,
        grid=(n // tm, IN_TOTAL // tn),
        in_specs=[
            pl.BlockSpec((tm, D_MODEL), lambda i, j: (i, 0)),
            pl.BlockSpec((1, D_MODEL), lambda i, j: (0, 0)),
            pl.BlockSpec((D_MODEL, tn), lambda i, j: (0, j)),
        ],
        out_specs=pl.BlockSpec((tm, tn), lambda i, j: (i, j)),
        out_shape=jax.ShapeDtypeStruct((n, IN_TOTAL), F32),
        scratch_shapes=[pltpu.VMEM((tm, D_MODEL), BF16)],
        compiler_params=pltpu.CompilerParams(
            dimension_semantics=("arbitrary", "arbitrary"), vmem_limit_bytes=VMEM_LIMIT),
        name="inproj",
    )(x, g, w_bf16)


def _cast_kernel(x_ref, o_ref):
    o_ref[...] = x_ref[...].astype(BF16)


def _cast_bf16(a, rows):
    n, c = a.shape
    return pl.pallas_call(
        _cast_kernel,
        grid=(n // rows,),
        in_specs=[pl.BlockSpec((rows, c), lambda i: (i, 0))],
        out_specs=pl.BlockSpec((rows, c), lambda i: (i, 0)),
        out_shape=jax.ShapeDtypeStruct((n, c), BF16),
        compiler_params=pltpu.CompilerParams(dimension_semantics=("arbitrary",), vmem_limit_bytes=VMEM_LIMIT),
        name="cast_bf16",
    )(a)


def _cast_transpose_kernel(x_ref, o_ref):
    o_ref[0] = x_ref[...].T.astype(BF16)


def _cast_transpose_tiles(a, rows):
    n, c = a.shape
    return pl.pallas_call(
        _cast_transpose_kernel,
        grid=(n // rows,),
        in_specs=[pl.BlockSpec((rows, c), lambda i: (i, 0))],
        out_specs=pl.BlockSpec((1, c, rows), lambda i: (i, 0, 0)),
        out_shape=jax.ShapeDtypeStruct((n // rows, c, rows), BF16),
        compiler_params=pltpu.CompilerParams(dimension_semantics=("arbitrary",), vmem_limit_bytes=VMEM_LIMIT),
        name="cast_transpose_tiles",
    )(a)


def _mixer_kernel(nb, tt, carry, offset,
                  u_ref, x_ref, ha_ref, hb_ref, hc_ref, h0_ref, hd_ref,
                  caw_ref, pw_ref, ps_ref, ccw_ref, ccb_ref, wa_ref, ba_ref, wi_ref, bi_ref, lam_ref,
                  cdw_ref, cdb_ref, lng_ref, lnb_ref, bg_ref,
                  wpa_ref, wpb_ref, wpc_ref, wpd_ref, wout_ref,
                  xo_ref, sa_ref, sb_ref, sc_ref, hl_ref, sd_ref,
                  ea_ref, eb_ref, ec_ref, ed_ref, h_ref, a_ref, bx_ref, hs_ref, yb_ref, gate_ref):
    r = tt * nb
    step = pl.program_id(0)
    first = (step == 0) if carry else None

    def load_hist():
        ea_ref[0:(CONV_A - 1) * nb, :] = ha_ref[...]
        eb_ref[0:POOL_BUF * nb, :] = hb_ref[...]
        ec_ref[0:(CONV_C - 1) * nb, :] = hc_ref[...]
        ed_ref[0:(CONV_D - 1) * nb, :] = hd_ref[...]
        h_ref[...] = h0_ref[...]

    if carry:
        pl.when(first)(load_hist)
    else:
        load_hist()

    z = u_ref[:, W_A:2 * W_A] * u_ref[:, 2 * W_A:3 * W_A]
    ea_ref[(CONV_A - 1) * nb:, :] = z
    y = caw_ref[0:1, :] * ea_ref[0:r, :]
    for k in range(1, CONV_A):
        y = y + caw_ref[k:k + 1, :] * ea_ref[k * nb:k * nb + r, :]
    ya = u_ref[:, 0:W_A] * y

    eb_ref[POOL_BUF * nb:, :] = u_ref[:, OFF_B:OFF_C]
    row = lax.broadcasted_iota(jnp.int32, (r, POOL_GROUP), 0)
    tpos = (row >> (nb.bit_length() - 1)) + (step * tt if carry else 0) + offset
    for g, w in enumerate(POOL_WINDOWS):
        sl = slice(g * POOL_GROUP, (g + 1) * POOL_GROUP)
        acc = eb_ref[POOL_BUF * nb:POOL_BUF * nb + r, sl]
        for k in range(1, w):
            acc = acc + eb_ref[(POOL_BUF - k) * nb:(POOL_BUF - k) * nb + r, sl]
        cnt = jnp.minimum(tpos + 1, w).astype(F32)
        p = acc / cnt - eb_ref[POOL_BUF * nb:POOL_BUF * nb + r, sl]
        yb_ref[:, sl] = _bdot(p, pw_ref[g]) * ps_ref[:, sl]

    ec_ref[(CONV_C - 1) * nb:, :] = u_ref[:, OFF_C:OFF_C + W_C]
    xconv = ccw_ref[0:1, :] * ec_ref[0:r, :]
    for k in range(1, CONV_C):
        xconv = xconv + ccw_ref[k:k + 1, :] * ec_ref[k * nb:k * nb + r, :]
    xconv = xconv + ccb_ref[...]
    lam = lam_ref[...]
    nsp = -LRU_C * (jnp.maximum(-lam, 0.0) + jnp.log(1.0 + jnp.exp(-jnp.abs(lam))))
    for hh in range(LRU_HEADS):
        sl = slice(hh * LRU_HEAD_DIM, (hh + 1) * LRU_HEAD_DIM)
        xh = xconv[:, sl]
        xhb = xh.astype(BF16)
        rg = _sigmoid(jnp.dot(xhb, wa_ref[hh], preferred_element_type=F32) + ba_ref[:, sl])
        ig = _sigmoid(jnp.dot(xhb, wi_ref[hh], preferred_element_type=F32) + bi_ref[:, sl])
        log_a = rg * nsp[:, sl]
        a = jnp.exp(log_a)
        a_ref[:, sl] = a
        bx_ref[:, sl] = jnp.sqrt(1.0 - a * a) * (ig * xh)

    if tt <= 8:
        for t in range(tt):
            h = a_ref[t * nb:(t + 1) * nb, :] * h_ref[...] + bx_ref[t * nb:(t + 1) * nb, :]
            h_ref[...] = h
            hs_ref[t * nb:(t + 1) * nb, :] = h
    else:
        def scan_body(t, h):
            rows = pl.ds(pl.multiple_of(t * nb, nb), nb)
            h = a_ref[rows, :] * h + bx_ref[rows, :]
            hs_ref[rows, :] = h
            return h
        h_ref[...] = lax.fori_loop(0, tt, scan_body, h_ref[...])
    yc = hs_ref[...] * _gelu(u_ref[:, OFF_C + W_C:OFF_D])

    ed_ref[(CONV_D - 1) * nb:, :] = u_ref[:, OFF_D:OFF_D + W_D] * _sigmoid(u_ref[:, OFF_D + W_D:OFF_G])
    c = cdw_ref[0:1, :] * ed_ref[0:r, :]
    for k in range(1, CONV_D):
        c = c + cdw_ref[k:k + 1, :] * ed_ref[k * nb:k * nb + r, :]
    c = c + cdb_ref[...]
    mu = jnp.mean(c, axis=-1, keepdims=True)
    cc = c - mu
    var = jnp.mean(cc * cc, axis=-1, keepdims=True)
    ln = cc * lax.rsqrt(var + EPS) * lng_ref[...] + lnb_ref[...]
    yd = ln * _sigmoid(ln)

    gate_ref[...] = _sigmoid(u_ref[:, OFF_G:] + bg_ref[...])
    m = gate_ref[:, 0:D_MODEL] * _bdot(ya, wpa_ref[...])
    m = m + gate_ref[:, D_MODEL:2 * D_MODEL] * _bdot(yb_ref[...], wpb_ref[...])
    m = m + gate_ref[:, 2 * D_MODEL:3 * D_MODEL] * _bdot(yc, wpc_ref[...])
    m = m + gate_ref[:, 3 * D_MODEL:] * _bdot(yd, wpd_ref[...])
    xo_ref[...] = x_ref[...] + _bdot(m, wout_ref[...])

    def store_state():
        sa_ref[...] = ea_ref[r:r + (CONV_A - 1) * nb, :]
        sb_ref[...] = eb_ref[r:r + POOL_BUF * nb, :]
        sc_ref[...] = ec_ref[r:r + (CONV_C - 1) * nb, :]
        sd_ref[...] = ed_ref[r:r + (CONV_D - 1) * nb, :]
        hl_ref[...] = h_ref[...]

    if carry:
        pl.when(step == pl.num_programs(0) - 1)(store_state)
        for e_ref, hist in ((ea_ref, CONV_A - 1), (eb_ref, POOL_BUF), (ec_ref, CONV_C - 1), (ed_ref, CONV_D - 1)):
            total = hist * nb
            for s in range(0, total, r):
                n = min(r, total - s)
                e_ref[s:s + n, :] = e_ref[r + s:r + s + n, :]
    else:
        store_state()


def _mixer(u, x, hists, lp, nb, tt, carry, offset):
    n = x.shape[0]
    r = nb * tt
    steps = n // r
    ha, hb, hc, h0, hd = hists
    widths = (W_A, W_B, W_C, W_D)
    hist_rows = ((CONV_A - 1) * nb, POOL_BUF * nb, (CONV_C - 1) * nb, (CONV_D - 1) * nb)

    def chunk_map(i):
        return (0, 0) if carry else (i, 0)

    def const2(i):
        return (0, 0)

    def const3(i):
        return (0, 0, 0)

    def full(a):
        return pl.BlockSpec(a.shape, const2 if a.ndim == 2 else const3)

    params = [lp['conv_a_w'], lp['pool_w'], lp['pool_scale'], lp['conv_c_w'], lp['conv_c_b'],
              lp['lru_wa'], lp['lru_ba'], lp['lru_wi'], lp['lru_bi'], lp['lru_lambda'],
              lp['conv_d_w'], lp['conv_d_b'], lp['ln_d_g'], lp['ln_d_b'], lp['b_gate'],
              lp['w_pa'], lp['w_pb'], lp['w_pc'], lp['w_pd'], lp['w_out']]
    in_specs = [
        pl.BlockSpec((r, IN_TOTAL), lambda i: (i, 0)),
        pl.BlockSpec((r, D_MODEL), lambda i: (i, 0)),
        pl.BlockSpec((hist_rows[0], W_A), chunk_map),
        pl.BlockSpec((hist_rows[1], W_B), chunk_map),
        pl.BlockSpec((hist_rows[2], W_C), chunk_map),
        pl.BlockSpec((nb, W_C), chunk_map),
        pl.BlockSpec((hist_rows[3], W_D), chunk_map),
    ] + [full(p) for p in params]
    n_state = 1 if carry else steps
    out_shape = (
        jax.ShapeDtypeStruct((n, D_MODEL), F32),
        jax.ShapeDtypeStruct((n_state * hist_rows[0], W_A), F32),
        jax.ShapeDtypeStruct((n_state * hist_rows[1], W_B), F32),
        jax.ShapeDtypeStruct((n_state * hist_rows[2], W_C), F32),
        jax.ShapeDtypeStruct((n_state * nb, W_C), F32),
        jax.ShapeDtypeStruct((n_state * hist_rows[3], W_D), F32),
    )
    out_specs = (
        pl.BlockSpec((r, D_MODEL), lambda i: (i, 0)),
        pl.BlockSpec((hist_rows[0], W_A), chunk_map),
        pl.BlockSpec((hist_rows[1], W_B), chunk_map),
        pl.BlockSpec((hist_rows[2], W_C), chunk_map),
        pl.BlockSpec((nb, W_C), chunk_map),
        pl.BlockSpec((hist_rows[3], W_D), chunk_map),
    )
    scratch = [pltpu.VMEM((hr + r, w), F32) for hr, w in zip(hist_rows, widths)]
    scratch += [
        pltpu.VMEM((nb, W_C), F32),
        pltpu.VMEM((r, W_C), F32),
        pltpu.VMEM((r, W_C), F32),
        pltpu.VMEM((r, W_C), F32),
        pltpu.VMEM((r, W_B), F32),
        pltpu.VMEM((r, 4 * D_MODEL), F32),
    ]
    return pl.pallas_call(
        functools.partial(_mixer_kernel, nb, tt, carry, offset),
        grid=(steps,),
        in_specs=in_specs,
        out_specs=out_specs,
        out_shape=out_shape,
        scratch_shapes=scratch,
        compiler_params=pltpu.CompilerParams(
            dimension_semantics=("arbitrary",), vmem_limit_bytes=VMEM_LIMIT),
        name="mixer_carry" if carry else "mixer_batch",
    )(u, x, ha, hb, hc, h0, hd, *params)


N_TOP = PEER_TOPK + 1
SV_ROWS = 24


PRO_LANES = 2 * LANES
SUBLANES = 8
MXU_COLS = 256


def _batcher_pairs(n):
    pairs = []

    def merge(lo, hi, r):
        step = r * 2
        if step < hi - lo:
            merge(lo, hi, step)
            merge(lo + r, hi, step)
            pairs.extend((i, i + r) for i in range(lo + r, hi - r, step))
        else:
            pairs.append((lo, lo + r))

    def sort(lo, hi):
        if hi - lo >= 1:
            mid = lo + (hi - lo) // 2
            sort(lo, mid)
            sort(mid + 1, hi)
            merge(lo, hi, 1)

    sort(0, n - 1)
    return pairs


def _sort_desc(ws):
    n = 1
    while n < len(ws):
        n *= 2
    ws = list(ws) + [None] * (n - len(ws))
    for i, j in _batcher_pairs(n):
        a, b = ws[i], ws[j]
        if b is None:
            continue
        if a is None:
            ws[i], ws[j] = b, None
        else:
            ws[i], ws[j] = jnp.maximum(a, b), jnp.minimum(a, b)
    return [w for w in ws if w is not None]


def _top_rows(ws, n):
    out = []
    for k in range(n):
        m = jnp.max(ws[0], axis=0, keepdims=True)
        out.append(m)
        remaining = n - 1 - k
        if remaining == 0:
            break
        hit = ws[0] >= m
        ws = [jnp.where(hit, ws[i + 1] if i + 1 < len(ws) else NEG_INF, ws[i])
              for i in range(min(len(ws), remaining))]
    return out


def _peer_kernel(tb, et, ts, final,
                 x_ref, g_ref, wqt_ref, keys_ref, u_ref, vt_ref, vtp_ref, fg_ref,
                 o_ref,
                 xnt_ref, qt_ref, s_ref, sv_ref, a_ref, b_ref, e2_ref, ht_ref, ct_ref, acc_ref):
    npl = tb // PRO_LANES
    rows_per_tile = et // N_KEYS
    j = pl.program_id(1)

    @pl.when(j == 0)
    def _():
        xn = _rms(x_ref[...], g_ref[...])
        xnt_ref[...] = xn.T.astype(BF16)
        qt_ref[...] = jnp.dot(wqt_ref[...], xnt_ref[...], preferred_element_type=F32).astype(BF16)
        for hc in range(2 * PEER_HEADS):
            s = jnp.dot(keys_ref[hc], qt_ref[hc * KEY_DIM:(hc + 1) * KEY_DIM, :],
                        preferred_element_type=F32)
            for pt in range(npl):
                s_ref[hc, pt] = s[:, pt * PRO_LANES:(pt + 1) * PRO_LANES]

        def select_body(idx, carry):
            h = idx // npl
            pt = idx % npl
            for hc in (2 * h, 2 * h + 1):
                ws = _sort_desc([s_ref[hc, pt, i * SUBLANES:(i + 1) * SUBLANES, :]
                                 for i in range(N_KEYS // SUBLANES)])
                rows = _top_rows(ws, N_TOP)
                sv_ref[hc, pt] = jnp.full((SV_ROWS, PRO_LANES), NEG_INF, F32)
                for k in range(N_TOP):
                    sv_ref[hc, pt, k:k + 1, :] = rows[k]
            sv1 = sv_ref[2 * h, pt]
            sv2 = sv_ref[2 * h + 1, pt]
            pieces = [sv1[0:1, :] + sv2[i * SUBLANES:(i + 1) * SUBLANES, :] for i in range(SV_ROWS // SUBLANES)]
            pieces += [sv1[a:a + 1, :] + sv2[0:SUBLANES, :] for a in range(1, SUBLANES)]
            pieces += [sv1[i * SUBLANES:(i + 1) * SUBLANES, :] + sv2[0:1, :] for i in range(1, SV_ROWS // SUBLANES)]
            best = _top_rows(_sort_desc(pieces), N_TOP)
            zsum = jnp.ones_like(best[0])
            for k in range(1, PEER_TOPK):
                zsum = zsum + jnp.exp(best[k] - best[0])
            thr = 0.5 * (best[PEER_TOPK - 1] + best[PEER_TOPK])
            s1 = s_ref[2 * h, pt]
            s2 = s_ref[2 * h + 1, pt]
            a_ref[h, pt] = jnp.exp(s1 - sv1[0:1, :]) / zsum
            b_ref[h, pt] = thr - s1
            e2_ref[h, pt] = jnp.exp(s2 - sv2[0:1, :])
            return carry

        lax.fori_loop(0, PEER_HEADS * npl, select_body, 0)
        acc_ref[...] = jnp.zeros_like(acc_ref)
        ct_ref[(ts - 1) * et:ts * et, tb - MXU_COLS:tb] = jnp.zeros((et, MXU_COLS), BF16)

    pieces = [(t, c) for t in range(ts) for c in range(tb // MXU_COLS)]
    units = [(t, c, rr) for (t, c) in pieces for rr in range(rows_per_tile)]
    out_rows = D_MODEL // rows_per_tile

    def pre_activations(t, c, rr):
        er = slice(t * et + rr * N_KEYS, t * et + (rr + 1) * N_KEYS)
        cols = slice(c * MXU_COLS, (c + 1) * MXU_COLS)
        ht_ref[er, cols] = jnp.dot(u_ref[er, :], xnt_ref[:, cols], preferred_element_type=F32)

    def output_term(t, c, m, v_ref=None):
        rows = slice(t * et, (t + 1) * et)
        cols = slice(c * MXU_COLS, (c + 1) * MXU_COLS)
        mr = slice(m * out_rows, (m + 1) * out_rows)
        v = vt_ref[t, mr, :] if v_ref is None else v_ref[0, mr, :]
        acc_ref[mr, cols] += jnp.dot(v, ct_ref[rows, cols], preferred_element_type=F32)

    def coefficient_tile(t, c, rr, l):
        i1 = (j * ts + t) * rows_per_tile + rr
        lt = c * (MXU_COLS // LANES) + l
        pt = lt // 2
        ln = slice((lt % 2) * LANES, (lt % 2 + 1) * LANES)
        g = jnp.zeros((N_KEYS, LANES), F32)
        for h in range(PEER_HEADS):
            a_row = a_ref[h, pt, pl.ds(i1, 1), :][:, ln]
            b_row = b_ref[h, pt, pl.ds(i1, 1), :][:, ln]
            g = g + jnp.where(s_ref[2 * h + 1, pt, :, ln] >= b_row, e2_ref[h, pt, :, ln], 0.0) * a_row
        er = slice(t * et + rr * N_KEYS, t * et + (rr + 1) * N_KEYS)
        tl = slice(lt * LANES, (lt + 1) * LANES)
        ct_ref[er, tl] = (g * _gelu_sigmoid_form(ht_ref[er, tl])).astype(BF16)

    pre_activations(*units[0])
    for n, (t, c, rr) in enumerate(units):
        if n + 1 < len(units):
            pre_activations(*units[n + 1])
        coefficient_tile(t, c, rr, 0)
        if n >= rows_per_tile:
            pt_, pc_, _ = units[n - rows_per_tile]
            output_term(pt_, pc_, rr)
        else:
            output_term(*pieces[-1], rr, v_ref=vtp_ref)
        coefficient_tile(t, c, rr, 1)

    @pl.when(j == pl.num_programs(1) - 1)
    def _():
        for m in range(rows_per_tile):
            output_term(*pieces[-1], m)
        y = x_ref[...] + acc_ref[...].T
        if final:
            y = _rms(y, fg_ref[...])
        o_ref[...] = y


def _peer(x, g, wqt, keys, u_tab, vt_tiles, final_g, final, tb, et, ts):
    n = x.shape[0]
    npl = tb // PRO_LANES
    return pl.pallas_call(
        functools.partial(_peer_kernel, tb, et, ts, final),
        grid=(n // tb, N_EXPERTS // (ts * et)),
        in_specs=[
            pl.BlockSpec((tb, D_MODEL), lambda i, j: (i, 0)),
            pl.BlockSpec((1, D_MODEL), lambda i, j: (0, 0)),
            pl.BlockSpec(wqt.shape, lambda i, j: (0, 0)),
            pl.BlockSpec(keys.shape, lambda i, j: (0, 0, 0)),
            pl.BlockSpec((ts * et, D_MODEL), lambda i, j: (j, 0)),
            pl.BlockSpec((ts, D_MODEL, et), lambda i, j: (j, 0, 0)),
            pl.BlockSpec((1, D_MODEL, et), lambda i, j: (jnp.maximum(j * ts - 1, 0), 0, 0)),
            pl.BlockSpec((1, D_MODEL), lambda i, j: (0, 0)),
        ],
        out_specs=pl.BlockSpec((tb, D_MODEL), lambda i, j: (i, 0)),
        out_shape=jax.ShapeDtypeStruct((n, D_MODEL), F32),
        scratch_shapes=[
            pltpu.VMEM((D_MODEL, tb), BF16),
            pltpu.VMEM((2 * PEER_HEADS * KEY_DIM, tb), BF16),
            pltpu.VMEM((2 * PEER_HEADS, npl, N_KEYS, PRO_LANES), F32),
            pltpu.VMEM((2 * PEER_HEADS, npl, SV_ROWS, PRO_LANES), F32),
            pltpu.VMEM((PEER_HEADS, npl, N_KEYS, PRO_LANES), F32),
            pltpu.VMEM((PEER_HEADS, npl, N_KEYS, PRO_LANES), F32),
            pltpu.VMEM((PEER_HEADS, npl, N_KEYS, PRO_LANES), F32),
            pltpu.VMEM((ts * et, tb), F32),
            pltpu.VMEM((ts * et, tb), BF16),
            pltpu.VMEM((D_MODEL, tb), F32),
        ],
        compiler_params=pltpu.CompilerParams(
            dimension_semantics=("arbitrary", "arbitrary"), vmem_limit_bytes=VMEM_LIMIT),
        name="peer_final" if final else "peer",
    )(x, g, wqt, keys, u_tab, vt_tiles, vt_tiles, final_g)


PROMPT_NB = 8
PROMPT_TT = 32
SAMPLE_NB = 32
CAST_ROWS_W_IN = 256
CAST_ROWS_TABLE = 2048
INPROJ_TM = 1024
INPROJ_TN = 1536
PEER_TB = 512
PEER_ET = 512
PEER_TS = 4


def _to_chunks(a, nb):
    b, t, c = a.shape
    return a.reshape(b // nb, nb, t, c).transpose(0, 2, 1, 3).reshape(b * t, c)


def _from_chunks(a, nb, b, t):
    c = a.shape[-1]
    return a.reshape(b // nb, t, nb, c).transpose(0, 2, 1, 3).reshape(b, t, c)


def kernel(x_prompt, x_sample, state_conv_a, state_pool, state_conv_c, state_lru_h, state_conv_d, norm1_g, w_in, b_gate, conv_a_w, pool_w, pool_scale, conv_c_w, conv_c_b, lru_wa, lru_ba, lru_wi, lru_bi, lru_lambda, conv_d_w, conv_d_b, ln_d_g, ln_d_b, w_pa, w_pb, w_pc, w_pd, w_out, norm2_g, peer_wq, peer_keys, peer_u, peer_v, final_g):
    bp, tp, _ = x_prompt.shape
    bs, ts, _ = x_sample.shape
    xp = _to_chunks(x_prompt, PROMPT_NB)
    xs = _to_chunks(x_sample, SAMPLE_NB)
    fg = final_g.reshape(1, D_MODEL)

    outs_p = [[] for _ in range(5)]
    outs_s = [[] for _ in range(5)]
    for l in range(DEPTH):
        lp = {
            'conv_a_w': conv_a_w[l], 'pool_w': pool_w[l].astype(BF16), 'pool_scale': pool_scale[l].reshape(1, W_B),
            'conv_c_w': conv_c_w[l], 'conv_c_b': conv_c_b[l].reshape(1, W_C),
            'lru_wa': lru_wa[l].astype(BF16), 'lru_ba': lru_ba[l].reshape(1, W_C),
            'lru_wi': lru_wi[l].astype(BF16), 'lru_bi': lru_bi[l].reshape(1, W_C),
            'lru_lambda': lru_lambda[l].reshape(1, W_C),
            'conv_d_w': conv_d_w[l], 'conv_d_b': conv_d_b[l].reshape(1, W_D),
            'ln_d_g': ln_d_g[l].reshape(1, W_D), 'ln_d_b': ln_d_b[l].reshape(1, W_D),
            'b_gate': b_gate[l].reshape(1, 4 * D_MODEL),
            'w_pa': w_pa[l].astype(BF16), 'w_pb': w_pb[l].astype(BF16), 'w_pc': w_pc[l].astype(BF16),
            'w_pd': w_pd[l].astype(BF16), 'w_out': w_out[l].astype(BF16),
        }
        g1 = norm1_g[l].reshape(1, D_MODEL)
        g2 = norm2_g[l].reshape(1, D_MODEL)
        w_in_b = _cast_bf16(w_in[l], CAST_ROWS_W_IN)
        wqt = peer_wq[l].T.astype(BF16)
        keys = peer_keys[l].reshape(2 * PEER_HEADS, N_KEYS, KEY_DIM).astype(BF16)
        u_tab = _cast_bf16(peer_u[l], CAST_ROWS_TABLE)
        vt_tab = _cast_transpose_tiles(peer_v[l], PEER_ET)
        final = l == DEPTH - 1

        zero_hist = (jnp.zeros(((CONV_A - 1) * bp, W_A), F32), jnp.zeros((POOL_BUF * bp, W_B), F32),
                     jnp.zeros(((CONV_C - 1) * bp, W_C), F32), jnp.zeros((bp, W_C), F32),
                     jnp.zeros(((CONV_D - 1) * bp, W_D), F32))
        u = _inproj(xp, g1, w_in_b, INPROJ_TM, INPROJ_TN)
        xp, *st = _mixer(u, xp, zero_hist, lp, PROMPT_NB, PROMPT_TT, True, 0)
        for acc, s in zip(outs_p, st):
            acc.append(s)
        xp = _peer(xp, g2, wqt, keys, u_tab, vt_tab, fg, final, PEER_TB, PEER_ET, PEER_TS)

        hist = (_to_chunks(state_conv_a[l], SAMPLE_NB), _to_chunks(state_pool[l], SAMPLE_NB),
                _to_chunks(state_conv_c[l], SAMPLE_NB), state_lru_h[l], _to_chunks(state_conv_d[l], SAMPLE_NB))
        u = _inproj(xs, g1, w_in_b, min(INPROJ_TM, xs.shape[0]), INPROJ_TN)
        xs, *st = _mixer(u, xs, hist, lp, SAMPLE_NB, ts, False, PAST_LEN)
        for acc, s in zip(outs_s, st):
            acc.append(s)
        xs = _peer(xs, g2, wqt, keys, u_tab, vt_tab, fg, final, PEER_TB, PEER_ET, PEER_TS)

    y_prompt = _from_chunks(xp, PROMPT_NB, bp, tp)
    y_sample = _from_chunks(xs, SAMPLE_NB, bs, ts)

    def states(outs, nb, b):
        res = []
        for idx, hist_len in ((0, CONV_A - 1), (1, POOL_BUF), (2, CONV_C - 1), (3, None), (4, CONV_D - 1)):
            if hist_len is None:
                res.append(jnp.stack(outs[idx]))
            else:
                res.append(jnp.stack([_from_chunks(s, nb, b, hist_len) for s in outs[idx]]))
        return res

    ca_p, pool_p, cc_p, h_p, cd_p = states(outs_p, PROMPT_NB, bp)
    ca_s, pool_s, cc_s, h_s, cd_s = states(outs_s, SAMPLE_NB, bs)
    return (y_prompt, y_sample, ca_p, ca_s, pool_p, pool_s, cc_p, cc_s, h_p, h_s, cd_p, cd_s)
```

```python
import functools

import jax
import jax.numpy as jnp
from jax import lax
from jax.experimental import pallas as pl
from jax.experimental.pallas import tpu as pltpu

F32 = jnp.float32
BF16 = jnp.bfloat16

D_MODEL = 1024
DEPTH = 2
PAST_LEN = 16384
W_A = 512
CONV_A = 3
W_B = 512
POOL_WINDOWS = (2, 4, 8, 16)
POOL_GROUP = 128
POOL_BUF = 15
W_C = 1024
LRU_HEADS = 8
LRU_HEAD_DIM = 128
CONV_C = 4
LRU_C = 8.0
W_D = 512
CONV_D = 31
OFF_B = 3 * W_A
OFF_C = OFF_B + W_B
OFF_D = OFF_C + 2 * W_C
OFF_G = OFF_D + 2 * W_D
IN_TOTAL = OFF_G + 4 * D_MODEL
PEER_HEADS = 8
N_KEYS = 128
N_EXPERTS = N_KEYS * N_KEYS
KEY_DIM = 128
PEER_TOPK = 16
EPS = 1e-6

LANES = 128
VMEM_LIMIT = 56 * 1024 * 1024

GELU_C0 = 0.7978845608028654
GELU_C1 = 0.044715
LOG2_E = 1.4426950408889634
NEG_INF = float("-inf")


def _sigmoid(x):
    return 1.0 / (1.0 + jnp.exp(-x))


def _gelu(x):
    return 0.5 * x * (1.0 + jnp.tanh(GELU_C0 * (x + GELU_C1 * (x * x * x))))


def _gelu_sigmoid_form(x):
    q = x * (x * x * (-2.0 * GELU_C0 * GELU_C1 * LOG2_E) + (-2.0 * GELU_C0 * LOG2_E))
    return x / (1.0 + jnp.exp2(q))


def _rms(x, g):
    ms = jnp.mean(x * x, axis=-1, keepdims=True)
    return x * lax.rsqrt(ms + EPS) * g


def _bdot(a, b):
    return jnp.dot(a.astype(BF16), b.astype(BF16), preferred_element_type=F32)


def _inproj_kernel(x_ref, g_ref, w_ref, u_ref, xn_ref):
    @pl.when(pl.program_id(1) == 0)
    def _():
        xn_ref[...] = _rms(x_ref[...], g_ref[...]).astype(BF16)

    u_ref[...] = jnp.dot(xn_ref[...], w_ref[...], preferred_element_type=F32)


def _inproj(x, g, w_bf16, tm, tn):
    n = x.shape[0]
    return pl.pallas_call(
        _inproj_kernel,
        grid=(n // tm, IN_TOTAL // tn),
        in_specs=[
            pl.BlockSpec((tm, D_MODEL), lambda i, j: (i, 0)),
            pl.BlockSpec((1, D_MODEL), lambda i, j: (0, 0)),
            pl.BlockSpec((D_MODEL, tn), lambda i, j: (0, j)),
        ],
        out_specs=pl.BlockSpec((tm, tn), lambda i, j: (i, j)),
        out_shape=jax.ShapeDtypeStruct((n, IN_TOTAL), F32),
        scratch_shapes=[pltpu.VMEM((tm, D_MODEL), BF16)],
        compiler_params=pltpu.CompilerParams(
            dimension_semantics=("arbitrary", "arbitrary"), vmem_limit_bytes=VMEM_LIMIT),
        name="inproj",
    )(x, g, w_bf16)


def _cast_kernel(x_ref, o_ref):
    o_ref[...] = x_ref[...].astype(BF16)


def _cast_bf16(a, layer, rows):
    _, n, c = a.shape
    return pl.pallas_call(
        _cast_kernel,
        grid=(n // rows,),
        in_specs=[pl.BlockSpec((None, rows, c), lambda i: (layer, i, 0))],
        out_specs=pl.BlockSpec((rows, c), lambda i: (i, 0)),
        out_shape=jax.ShapeDtypeStruct((n, c), BF16),
        compiler_params=pltpu.CompilerParams(dimension_semantics=("arbitrary",), vmem_limit_bytes=VMEM_LIMIT),
        name="cast_bf16",
    )(a)


def _cast_transpose_kernel(x_ref, o_ref):
    o_ref[0] = x_ref[...].T.astype(BF16)


def _cast_transpose_tiles(a, layer, rows):
    _, n, c = a.shape
    return pl.pallas_call(
        _cast_transpose_kernel,
        grid=(n // rows,),
        in_specs=[pl.BlockSpec((None, rows, c), lambda i: (layer, i, 0))],
        out_specs=pl.BlockSpec((1, c, rows), lambda i: (i, 0, 0)),
        out_shape=jax.ShapeDtypeStruct((n // rows, c, rows), BF16),
        compiler_params=pltpu.CompilerParams(dimension_semantics=("arbitrary",), vmem_limit_bytes=VMEM_LIMIT),
        name="cast_transpose_tiles",
    )(a)


def _mixer_kernel(nb, tt, carry, offset,
                  u_ref, x_ref, ha_ref, hb_ref, hc_ref, h0_ref, hd_ref,
                  caw_ref, pw_ref, ps_ref, ccw_ref, ccb_ref, wa_ref, ba_ref, wi_ref, bi_ref, lam_ref,
                  cdw_ref, cdb_ref, lng_ref, lnb_ref, bg_ref,
                  wpa_ref, wpb_ref, wpc_ref, wpd_ref, wout_ref,
                  xo_ref, sa_ref, sb_ref, sc_ref, hl_ref, sd_ref,
                  ea_ref, eb_ref, ec_ref, ed_ref, h_ref, a_ref, bx_ref, hs_ref, yb_ref, gate_ref):
    r = tt * nb
    step = pl.program_id(0)
    first = (step == 0) if carry else None

    def load_hist():
        ea_ref[0:(CONV_A - 1) * nb, :] = ha_ref[...]
        eb_ref[0:POOL_BUF * nb, :] = hb_ref[...]
        ec_ref[0:(CONV_C - 1) * nb, :] = hc_ref[...]
        ed_ref[0:(CONV_D - 1) * nb, :] = hd_ref[...]
        h_ref[...] = h0_ref[...]

    if carry:
        pl.when(first)(load_hist)
    else:
        load_hist()

    z = u_ref[:, W_A:2 * W_A] * u_ref[:, 2 * W_A:3 * W_A]
    ea_ref[(CONV_A - 1) * nb:, :] = z
    y = caw_ref[0:1, :] * ea_ref[0:r, :]
    for k in range(1, CONV_A):
        y = y + caw_ref[k:k + 1, :] * ea_ref[k * nb:k * nb + r, :]
    ya = u_ref[:, 0:W_A] * y

    eb_ref[POOL_BUF * nb:, :] = u_ref[:, OFF_B:OFF_C]
    row = lax.broadcasted_iota(jnp.int32, (r, POOL_GROUP), 0)
    tpos = (row >> (nb.bit_length() - 1)) + (step * tt if carry else 0) + offset
    for g, w in enumerate(POOL_WINDOWS):
        sl = slice(g * POOL_GROUP, (g + 1) * POOL_GROUP)
        acc = eb_ref[POOL_BUF * nb:POOL_BUF * nb + r, sl]
        for k in range(1, w):
            acc = acc + eb_ref[(POOL_BUF - k) * nb:(POOL_BUF - k) * nb + r, sl]
        cnt = jnp.minimum(tpos + 1, w).astype(F32)
        p = acc / cnt - eb_ref[POOL_BUF * nb:POOL_BUF * nb + r, sl]
        yb_ref[:, sl] = _bdot(p, pw_ref[g]) * ps_ref[:, sl]

    ec_ref[(CONV_C - 1) * nb:, :] = u_ref[:, OFF_C:OFF_C + W_C]
    xconv = ccw_ref[0:1, :] * ec_ref[0:r, :]
    for k in range(1, CONV_C):
        xconv = xconv + ccw_ref[k:k + 1, :] * ec_ref[k * nb:k * nb + r, :]
    xconv = xconv + ccb_ref[...]
    lam = lam_ref[...]
    nsp = -LRU_C * (jnp.maximum(-lam, 0.0) + jnp.log(1.0 + jnp.exp(-jnp.abs(lam))))
    for hh in range(LRU_HEADS):
        sl = slice(hh * LRU_HEAD_DIM, (hh + 1) * LRU_HEAD_DIM)
        xh = xconv[:, sl]
        xhb = xh.astype(BF16)
        rg = _sigmoid(jnp.dot(xhb, wa_ref[hh], preferred_element_type=F32) + ba_ref[:, sl])
        ig = _sigmoid(jnp.dot(xhb, wi_ref[hh], preferred_element_type=F32) + bi_ref[:, sl])
        log_a = rg * nsp[:, sl]
        a = jnp.exp(log_a)
        a_ref[:, sl] = a
        bx_ref[:, sl] = jnp.sqrt(1.0 - a * a) * (ig * xh)

    if tt <= 8:
        for t in range(tt):
            h = a_ref[t * nb:(t + 1) * nb, :] * h_ref[...] + bx_ref[t * nb:(t + 1) * nb, :]
            h_ref[...] = h
            hs_ref[t * nb:(t + 1) * nb, :] = h
    else:
        def scan_body(t, h):
            rows = pl.ds(pl.multiple_of(t * nb, nb), nb)
            h = a_ref[rows, :] * h + bx_ref[rows, :]
            hs_ref[rows, :] = h
            return h
        h_ref[...] = lax.fori_loop(0, tt, scan_body, h_ref[...])
    yc = hs_ref[...] * _gelu(u_ref[:, OFF_C + W_C:OFF_D])

    ed_ref[(CONV_D - 1) * nb:, :] = u_ref[:, OFF_D:OFF_D + W_D] * _sigmoid(u_ref[:, OFF_D + W_D:OFF_G])
    c = cdw_ref[0:1, :] * ed_ref[0:r, :]
    for k in range(1, CONV_D):
        c = c + cdw_ref[k:k + 1, :] * ed_ref[k * nb:k * nb + r, :]
    c = c + cdb_ref[...]
    mu = jnp.mean(c, axis=-1, keepdims=True)
    cc = c - mu
    var = jnp.mean(cc * cc, axis=-1, keepdims=True)
    ln = cc * lax.rsqrt(var + EPS) * lng_ref[...] + lnb_ref[...]
    yd = ln * _sigmoid(ln)

    gate_ref[...] = _sigmoid(u_ref[:, OFF_G:] + bg_ref[...])
    m = gate_ref[:, 0:D_MODEL] * _bdot(ya, wpa_ref[...])
    m = m + gate_ref[:, D_MODEL:2 * D_MODEL] * _bdot(yb_ref[...], wpb_ref[...])
    m = m + gate_ref[:, 2 * D_MODEL:3 * D_MODEL] * _bdot(yc, wpc_ref[...])
    m = m + gate_ref[:, 3 * D_MODEL:] * _bdot(yd, wpd_ref[...])
    xo_ref[...] = x_ref[...] + _bdot(m, wout_ref[...])

    def store_state():
        sa_ref[...] = ea_ref[r:r + (CONV_A - 1) * nb, :]
        sb_ref[...] = eb_ref[r:r + POOL_BUF * nb, :]
        sc_ref[...] = ec_ref[r:r + (CONV_C - 1) * nb, :]
        sd_ref[...] = ed_ref[r:r + (CONV_D - 1) * nb, :]
        hl_ref[...] = h_ref[...]

    if carry:
        pl.when(step == pl.num_programs(0) - 1)(store_state)
        for e_ref, hist in ((ea_ref, CONV_A - 1), (eb_ref, POOL_BUF), (ec_ref, CONV_C - 1), (ed_ref, CONV_D - 1)):
            total = hist * nb
            for s in range(0, total, r):
                n = min(r, total - s)
                e_ref[s:s + n, :] = e_ref[r + s:r + s + n, :]
    else:
        store_state()


def _mixer(u, x, hists, lp, nb, tt, carry, offset):
    n = x.shape[0]
    r = nb * tt
    steps = n // r
    ha, hb, hc, h0, hd = hists
    widths = (W_A, W_B, W_C, W_D)
    hist_rows = ((CONV_A - 1) * nb, POOL_BUF * nb, (CONV_C - 1) * nb, (CONV_D - 1) * nb)

    def chunk_map(i):
        return (0, 0) if carry else (i, 0)

    def const2(i):
        return (0, 0)

    def const3(i):
        return (0, 0, 0)

    def full(a):
        return pl.BlockSpec(a.shape, const2 if a.ndim == 2 else const3)

    params = [lp['conv_a_w'], lp['pool_w'], lp['pool_scale'], lp['conv_c_w'], lp['conv_c_b'],
              lp['lru_wa'], lp['lru_ba'], lp['lru_wi'], lp['lru_bi'], lp['lru_lambda'],
              lp['conv_d_w'], lp['conv_d_b'], lp['ln_d_g'], lp['ln_d_b'], lp['b_gate'],
              lp['w_pa'], lp['w_pb'], lp['w_pc'], lp['w_pd'], lp['w_out']]
    in_specs = [
        pl.BlockSpec((r, IN_TOTAL), lambda i: (i, 0)),
        pl.BlockSpec((r, D_MODEL), lambda i: (i, 0)),
        pl.BlockSpec((hist_rows[0], W_A), chunk_map),
        pl.BlockSpec((hist_rows[1], W_B), chunk_map),
        pl.BlockSpec((hist_rows[2], W_C), chunk_map),
        pl.BlockSpec((nb, W_C), chunk_map),
        pl.BlockSpec((hist_rows[3], W_D), chunk_map),
    ] + [full(p) for p in params]
    n_state = 1 if carry else steps
    out_shape = (
        jax.ShapeDtypeStruct((n, D_MODEL), F32),
        jax.ShapeDtypeStruct((n_state * hist_rows[0], W_A), F32),
        jax.ShapeDtypeStruct((n_state * hist_rows[1], W_B), F32),
        jax.ShapeDtypeStruct((n_state * hist_rows[2], W_C), F32),
        jax.ShapeDtypeStruct((n_state * nb, W_C), F32),
        jax.ShapeDtypeStruct((n_state * hist_rows[3], W_D), F32),
    )
    out_specs = (
        pl.BlockSpec((r, D_MODEL), lambda i: (i, 0)),
        pl.BlockSpec((hist_rows[0], W_A), chunk_map),
        pl.BlockSpec((hist_rows[1], W_B), chunk_map),
        pl.BlockSpec((hist_rows[2], W_C), chunk_map),
        pl.BlockSpec((nb, W_C), chunk_map),
        pl.BlockSpec((hist_rows[3], W_D), chunk_map),
    )
    scratch = [pltpu.VMEM((hr + r, w), F32) for hr, w in zip(hist_rows, widths)]
    scratch += [
        pltpu.VMEM((nb, W_C), F32),
        pltpu.VMEM((r, W_C), F32),
        pltpu.VMEM((r, W_C), F32),
        pltpu.VMEM((r, W_C), F32),
        pltpu.VMEM((r, W_B), F32),
        pltpu.VMEM((r, 4 * D_MODEL), F32),
    ]
    return pl.pallas_call(
        functools.partial(_mixer_kernel, nb, tt, carry, offset),
        grid=(steps,),
        in_specs=in_specs,
        out_specs=out_specs,
        out_shape=out_shape,
        scratch_shapes=scratch,
        compiler_params=pltpu.CompilerParams(
            dimension_semantics=("arbitrary",), vmem_limit_bytes=VMEM_LIMIT),
        name="mixer_carry" if carry else "mixer_batch",
    )(u, x, ha, hb, hc, h0, hd, *params)


N_TOP = PEER_TOPK + 1
SV_ROWS = 24


PRO_LANES = 2 * LANES
SUBLANES = 8
MXU_COLS = 256


def _batcher_pairs(n):
    pairs = []

    def merge(lo, hi, r):
        step = r * 2
        if step < hi - lo:
            merge(lo, hi, step)
            merge(lo + r, hi, step)
            pairs.extend((i, i + r) for i in range(lo + r, hi - r, step))
        else:
            pairs.append((lo, lo + r))

    def sort(lo, hi):
        if hi - lo >= 1:
            mid = lo + (hi - lo) // 2
            sort(lo, mid)
            sort(mid + 1, hi)
            merge(lo, hi, 1)

    sort(0, n - 1)
    return pairs


def _sort_desc(ws):
    n = 1
    while n < len(ws):
        n *= 2
    ws = list(ws) + [None] * (n - len(ws))
    for i, j in _batcher_pairs(n):
        a, b = ws[i], ws[j]
        if b is None:
            continue
        if a is None:
            ws[i], ws[j] = b, None
        else:
            ws[i], ws[j] = jnp.maximum(a, b), jnp.minimum(a, b)
    return [w for w in ws if w is not None]


def _top_rows(ws, n):
    out = []
    for k in range(n):
        m = jnp.max(ws[0], axis=0, keepdims=True)
        out.append(m)
        remaining = n - 1 - k
        if remaining == 0:
            break
        hit = ws[0] >= m
        ws = [jnp.where(hit, ws[i + 1] if i + 1 < len(ws) else NEG_INF, ws[i])
              for i in range(min(len(ws), remaining))]
    return out


def _peer_kernel(tb, et, ts, final,
                 x_ref, g_ref, wqt_ref, keys_ref, u_ref, vt_ref, vtp_ref, fg_ref,
                 o_ref,
                 xnt_ref, qt_ref, s_ref, sv_ref, a_ref, b_ref, e2_ref, ht_ref, ct_ref, acc_ref):
    npl = tb // PRO_LANES
    rows_per_tile = et // N_KEYS
    j = pl.program_id(1)

    @pl.when(j == 0)
    def _():
        xn = _rms(x_ref[...], g_ref[...])
        xnt_ref[...] = xn.T.astype(BF16)
        qt_ref[...] = jnp.dot(wqt_ref[...], xnt_ref[...], preferred_element_type=F32).astype(BF16)
        for hc in range(2 * PEER_HEADS):
            s = jnp.dot(keys_ref[hc], qt_ref[hc * KEY_DIM:(hc + 1) * KEY_DIM, :],
                        preferred_element_type=F32)
            for pt in range(npl):
                s_ref[hc, pt] = s[:, pt * PRO_LANES:(pt + 1) * PRO_LANES]

        def select_body(idx, carry):
            h = idx // npl
            pt = idx % npl
            for hc in (2 * h, 2 * h + 1):
                ws = _sort_desc([s_ref[hc, pt, i * SUBLANES:(i + 1) * SUBLANES, :]
                                 for i in range(N_KEYS // SUBLANES)])
                rows = _top_rows(ws, N_TOP)
                sv_ref[hc, pt] = jnp.full((SV_ROWS, PRO_LANES), NEG_INF, F32)
                for k in range(N_TOP):
                    sv_ref[hc, pt, k:k + 1, :] = rows[k]
            sv1 = sv_ref[2 * h, pt]
            sv2 = sv_ref[2 * h + 1, pt]
            pieces = [sv1[0:1, :] + sv2[i * SUBLANES:(i + 1) * SUBLANES, :] for i in range(SV_ROWS // SUBLANES)]
            pieces += [sv1[a:a + 1, :] + sv2[0:SUBLANES, :] for a in range(1, SUBLANES)]
            pieces += [sv1[i * SUBLANES:(i + 1) * SUBLANES, :] + sv2[0:1, :] for i in range(1, SV_ROWS // SUBLANES)]
            best = _top_rows(_sort_desc(pieces), N_TOP)
            zsum = jnp.ones_like(best[0])
            for k in range(1, PEER_TOPK):
                zsum = zsum + jnp.exp(best[k] - best[0])
            thr = 0.5 * (best[PEER_TOPK - 1] + best[PEER_TOPK])
            s1 = s_ref[2 * h, pt]
            s2 = s_ref[2 * h + 1, pt]
            a_ref[h, pt] = jnp.exp(s1 - sv1[0:1, :]) / zsum
            b_ref[h, pt] = thr - s1
            e2_ref[h, pt] = jnp.exp(s2 - sv2[0:1, :])
            return carry

        lax.fori_loop(0, PEER_HEADS * npl, select_body, 0)
        acc_ref[...] = jnp.zeros_like(acc_ref)
        ct_ref[(ts - 1) * et:ts * et, tb - MXU_COLS:tb] = jnp.zeros((et, MXU_COLS), BF16)

    pieces = [(t, c) for t in range(ts) for c in range(tb // MXU_COLS)]
    units = [(t, c, rr) for (t, c) in pieces for rr in range(rows_per_tile)]
    out_rows = D_MODEL // rows_per_tile

    def pre_activations(t, c, rr):
        er = slice(t * et + rr * N_KEYS, t * et + (rr + 1) * N_KEYS)
        cols = slice(c * MXU_COLS, (c + 1) * MXU_COLS)
        ht_ref[er, cols] = jnp.dot(u_ref[er, :], xnt_ref[:, cols], preferred_element_type=F32)

    def output_term(t, c, m, v_ref=None):
        rows = slice(t * et, (t + 1) * et)
        cols = slice(c * MXU_COLS, (c + 1) * MXU_COLS)
        mr = slice(m * out_rows, (m + 1) * out_rows)
        v = vt_ref[t, mr, :] if v_ref is None else v_ref[0, mr, :]
        acc_ref[mr, cols] += jnp.dot(v, ct_ref[rows, cols], preferred_element_type=F32)

    def coefficient_tile(t, c, rr, l):
        i1 = (j * ts + t) * rows_per_tile + rr
        lt = c * (MXU_COLS // LANES) + l
        pt = lt // 2
        ln = slice((lt % 2) * LANES, (lt % 2 + 1) * LANES)
        g = jnp.zeros((N_KEYS, LANES), F32)
        for h in range(PEER_HEADS):
            a_row = a_ref[h, pt, pl.ds(i1, 1), :][:, ln]
            b_row = b_ref[h, pt, pl.ds(i1, 1), :][:, ln]
            g = g + jnp.where(s_ref[2 * h + 1, pt, :, ln] >= b_row, e2_ref[h, pt, :, ln], 0.0) * a_row
        er = slice(t * et + rr * N_KEYS, t * et + (rr + 1) * N_KEYS)
        tl = slice(lt * LANES, (lt + 1) * LANES)
        ct_ref[er, tl] = (g * _gelu_sigmoid_form(ht_ref[er, tl])).astype(BF16)

    pre_activations(*units[0])
    for n, (t, c, rr) in enumerate(units):
        if n + 1 < len(units):
            pre_activations(*units[n + 1])
        coefficient_tile(t, c, rr, 0)
        if n >= rows_per_tile:
            pt_, pc_, _ = units[n - rows_per_tile]
            output_term(pt_, pc_, rr)
        else:
            output_term(*pieces[-1], rr, v_ref=vtp_ref)
        coefficient_tile(t, c, rr, 1)

    @pl.when(j == pl.num_programs(1) - 1)
    def _():
        for m in range(rows_per_tile):
            output_term(*pieces[-1], m)
        y = x_ref[...] + acc_ref[...].T
        if final:
            y = _rms(y, fg_ref[...])
        o_ref[...] = y


def _peer(x, g, wqt, keys, u_tab, vt_tiles, final_g, final, tb, et, ts):
    n = x.shape[0]
    npl = tb // PRO_LANES
    return pl.pallas_call(
        functools.partial(_peer_kernel, tb, et, ts, final),
        grid=(n // tb, N_EXPERTS // (ts * et)),
        in_specs=[
            pl.BlockSpec((tb, D_MODEL), lambda i, j: (i, 0)),
            pl.BlockSpec((1, D_MODEL), lambda i, j: (0, 0)),
            pl.BlockSpec(wqt.shape, lambda i, j: (0, 0)),
            pl.BlockSpec(keys.shape, lambda i, j: (0, 0, 0)),
            pl.BlockSpec((ts * et, D_MODEL), lambda i, j: (j, 0)),
            pl.BlockSpec((ts, D_MODEL, et), lambda i, j: (j, 0, 0)),
            pl.BlockSpec((1, D_MODEL, et), lambda i, j: (jnp.maximum(j * ts - 1, 0), 0, 0)),
            pl.BlockSpec((1, D_MODEL), lambda i, j: (0, 0)),
        ],
        out_specs=pl.BlockSpec((tb, D_MODEL), lambda i, j: (i, 0)),
        out_shape=jax.ShapeDtypeStruct((n, D_MODEL), F32),
        scratch_shapes=[
            pltpu.VMEM((D_MODEL, tb), BF16),
            pltpu.VMEM((2 * PEER_HEADS * KEY_DIM, tb), BF16),
            pltpu.VMEM((2 * PEER_HEADS, npl, N_KEYS, PRO_LANES), F32),
            pltpu.VMEM((2 * PEER_HEADS, npl, SV_ROWS, PRO_LANES), F32),
            pltpu.VMEM((PEER_HEADS, npl, N_KEYS, PRO_LANES), F32),
            pltpu.VMEM((PEER_HEADS, npl, N_KEYS, PRO_LANES), F32),
            pltpu.VMEM((PEER_HEADS, npl, N_KEYS, PRO_LANES), F32),
            pltpu.VMEM((ts * et, tb), F32),
            pltpu.VMEM((ts * et, tb), BF16),
            pltpu.VMEM((D_MODEL, tb), F32),
        ],
        compiler_params=pltpu.CompilerParams(
            dimension_semantics=("arbitrary", "arbitrary"), vmem_limit_bytes=VMEM_LIMIT),
        name="peer_final" if final else "peer",
    )(x, g, wqt, keys, u_tab, vt_tiles, vt_tiles, final_g)


PROMPT_NB = 8
PROMPT_TT = 32
SAMPLE_NB = 32
CAST_ROWS_W_IN = 256
CAST_ROWS_TABLE = 2048
INPROJ_TM = 1024
INPROJ_TN = 1536
PEER_TB = 512
PEER_ET = 512
PEER_TS = 4


def _to_chunks(a, nb):
    b, t, c = a.shape
    return a.reshape(b // nb, nb, t, c).transpose(0, 2, 1, 3).reshape(b * t, c)


def _from_chunks(a, nb, b, t):
    c = a.shape[-1]
    return a.reshape(b // nb, t, nb, c).transpose(0, 2, 1, 3).reshape(b, t, c)


def kernel(x_prompt, x_sample, state_conv_a, state_pool, state_conv_c, state_lru_h, state_conv_d, norm1_g, w_in, b_gate, conv_a_w, pool_w, pool_scale, conv_c_w, conv_c_b, lru_wa, lru_ba, lru_wi, lru_bi, lru_lambda, conv_d_w, conv_d_b, ln_d_g, ln_d_b, w_pa, w_pb, w_pc, w_pd, w_out, norm2_g, peer_wq, peer_keys, peer_u, peer_v, final_g):
    bp, tp, _ = x_prompt.shape
    bs, ts, _ = x_sample.shape
    xp = _to_chunks(x_prompt, PROMPT_NB)
    xs = _to_chunks(x_sample, SAMPLE_NB)
    fg = final_g.reshape(1, D_MODEL)

    outs_p = [[] for _ in range(5)]
    outs_s = [[] for _ in range(5)]
    for l in range(DEPTH):
        lp = {
            'conv_a_w': conv_a_w[l], 'pool_w': pool_w[l].astype(BF16), 'pool_scale': pool_scale[l].reshape(1, W_B),
            'conv_c_w': conv_c_w[l], 'conv_c_b': conv_c_b[l].reshape(1, W_C),
            'lru_wa': lru_wa[l].astype(BF16), 'lru_ba': lru_ba[l].reshape(1, W_C),
            'lru_wi': lru_wi[l].astype(BF16), 'lru_bi': lru_bi[l].reshape(1, W_C),
            'lru_lambda': lru_lambda[l].reshape(1, W_C),
            'conv_d_w': conv_d_w[l], 'conv_d_b': conv_d_b[l].reshape(1, W_D),
            'ln_d_g': ln_d_g[l].reshape(1, W_D), 'ln_d_b': ln_d_b[l].reshape(1, W_D),
            'b_gate': b_gate[l].reshape(1, 4 * D_MODEL),
            'w_pa': w_pa[l].astype(BF16), 'w_pb': w_pb[l].astype(BF16), 'w_pc': w_pc[l].astype(BF16),
            'w_pd': w_pd[l].astype(BF16), 'w_out': w_out[l].astype(BF16),
        }
        g1 = norm1_g[l].reshape(1, D_MODEL)
        g2 = norm2_g[l].reshape(1, D_MODEL)
        w_in_b = _cast_bf16(w_in, l, CAST_ROWS_W_IN)
        wqt = peer_wq[l].T.astype(BF16)
        keys = peer_keys[l].reshape(2 * PEER_HEADS, N_KEYS, KEY_DIM).astype(BF16)
        u_tab = _cast_bf16(peer_u, l, CAST_ROWS_TABLE)
        vt_tab = _cast_transpose_tiles(peer_v, l, PEER_ET)
        final = l == DEPTH - 1

        zero_hist = (jnp.zeros(((CONV_A - 1) * bp, W_A), F32), jnp.zeros((POOL_BUF * bp, W_B), F32),
                     jnp.zeros(((CONV_C - 1) * bp, W_C), F32), jnp.zeros((bp, W_C), F32),
                     jnp.zeros(((CONV_D - 1) * bp, W_D), F32))
        u = _inproj(xp, g1, w_in_b, INPROJ_TM, INPROJ_TN)
        xp, *st = _mixer(u, xp, zero_hist, lp, PROMPT_NB, PROMPT_TT, True, 0)
        for acc, s in zip(outs_p, st):
            acc.append(s)
        xp = _peer(xp, g2, wqt, keys, u_tab, vt_tab, fg, final, PEER_TB, PEER_ET, PEER_TS)

        hist = (_to_chunks(state_conv_a[l], SAMPLE_NB), _to_chunks(state_pool[l], SAMPLE_NB),
                _to_chunks(state_conv_c[l], SAMPLE_NB), state_lru_h[l], _to_chunks(state_conv_d[l], SAMPLE_NB))
        u = _inproj(xs, g1, w_in_b, min(INPROJ_TM, xs.shape[0]), INPROJ_TN)
        xs, *st = _mixer(u, xs, hist, lp, SAMPLE_NB, ts, False, PAST_LEN)
        for acc, s in zip(outs_s, st):
            acc.append(s)
        xs = _peer(xs, g2, wqt, keys, u_tab, vt_tab, fg, final, PEER_TB, PEER_ET, PEER_TS)

    y_prompt = _from_chunks(xp, PROMPT_NB, bp, tp)
    y_sample = _from_chunks(xs, SAMPLE_NB, bs, ts)

    def states(outs, nb, b):
        res = []
        for idx, hist_len in ((0, CONV_A - 1), (1, POOL_BUF), (2, CONV_C - 1), (3, None), (4, CONV_D - 1)):
            if hist_len is None:
                res.append(jnp.stack(outs[idx]))
            else:
                res.append(jnp.stack([_from_chunks(s, nb, b, hist_len) for s in outs[idx]]))
        return res

    ca_p, pool_p, cc_p, h_p, cd_p = states(outs_p, PROMPT_NB, bp)
    ca_s, pool_s, cc_s, h_s, cd_s = states(outs_s, SAMPLE_NB, bs)
    return (y_prompt, y_sample, ca_p, ca_s, pool_p, pool_s, cc_p, cc_s, h_p, h_s, cd_p, cd_s)
```

```python
import functools

import jax
import jax.numpy as jnp
from jax import lax
from jax.experimental import pallas as pl
from jax.experimental.pallas import tpu as pltpu

F32 = jnp.float32
BF16 = jnp.bfloat16

D_MODEL = 1024
DEPTH = 2
PAST_LEN = 16384
W_A = 512
CONV_A = 3
W_B = 512
POOL_WINDOWS = (2, 4, 8, 16)
POOL_GROUP = 128
POOL_BUF = 15
W_C = 1024
LRU_HEADS = 8
LRU_HEAD_DIM = 128
CONV_C = 4
LRU_C = 8.0
W_D = 512
CONV_D = 31
OFF_B = 3 * W_A
OFF_C = OFF_B + W_B
OFF_D = OFF_C + 2 * W_C
OFF_G = OFF_D + 2 * W_D
IN_TOTAL = OFF_G + 4 * D_MODEL
PEER_HEADS = 8
N_KEYS = 128
N_EXPERTS = N_KEYS * N_KEYS
KEY_DIM = 128
PEER_TOPK = 16
EPS = 1e-6

LANES = 128
VMEM_LIMIT = 56 * 1024 * 1024

GELU_C0 = 0.7978845608028654
GELU_C1 = 0.044715
LOG2_E = 1.4426950408889634
NEG_INF = float("-inf")


def _sigmoid(x):
    return 1.0 / (1.0 + jnp.exp(-x))


def _gelu(x):
    return 0.5 * x * (1.0 + jnp.tanh(GELU_C0 * (x + GELU_C1 * (x * x * x))))


def _gelu_sigmoid_form(x):
    q = x * (x * x * (-2.0 * GELU_C0 * GELU_C1 * LOG2_E) + (-2.0 * GELU_C0 * LOG2_E))
    return x / (1.0 + jnp.exp2(q))


def _rms(x, g):
    ms = jnp.mean(x * x, axis=-1, keepdims=True)
    return x * lax.rsqrt(ms + EPS) * g


def _bdot(a, b):
    return jnp.dot(a.astype(BF16), b.astype(BF16), preferred_element_type=F32)


def _inproj_kernel(x_ref, g_ref, w_ref, u_ref, xn_ref):
    @pl.when(pl.program_id(1) == 0)
    def _():
        xn_ref[...] = _rms(x_ref[...], g_ref[...]).astype(BF16)

    u_ref[...] = jnp.dot(xn_ref[...], w_ref[...], preferred_element_type=F32)


def _inproj(x, g, w_bf16, tm, tn):
    n = x.shape[0]
    return pl.pallas_call(
        _inproj_kernel,
        grid=(n // tm, IN_TOTAL // tn),
        in_specs=[
            pl.BlockSpec((tm, D_MODEL), lambda i, j: (i, 0)),
            pl.BlockSpec((1, D_MODEL), lambda i, j: (0, 0)),
            pl.BlockSpec((D_MODEL, tn), lambda i, j: (0, j)),
        ],
        out_specs=pl.BlockSpec((tm, tn), lambda i, j: (i, j)),
        out_shape=jax.ShapeDtypeStruct((n, IN_TOTAL), F32),
        scratch_shapes=[pltpu.VMEM((tm, D_MODEL), BF16)],
        compiler_params=pltpu.CompilerParams(
            dimension_semantics=("arbitrary", "arbitrary"), vmem_limit_bytes=VMEM_LIMIT),
        name="inproj",
    )(x, g, w_bf16)


def _cast_kernel(x_ref, o_ref):
    o_ref[...] = x_ref[...].astype(BF16)


def _cast_bf16(a, layer, rows):
    _, n, c = a.shape
    return pl.pallas_call(
        _cast_kernel,
        grid=(n // rows,),
        in_specs=[pl.BlockSpec((None, rows, c), lambda i: (layer, i, 0))],
        out_specs=pl.BlockSpec((rows, c), lambda i: (i, 0)),
        out_shape=jax.ShapeDtypeStruct((n, c), BF16),
        compiler_params=pltpu.CompilerParams(dimension_semantics=("arbitrary",), vmem_limit_bytes=VMEM_LIMIT),
        name="cast_bf16",
    )(a)


def _cast_transpose_kernel(x_ref, o_ref):
    o_ref[0] = x_ref[...].T.astype(BF16)


def _cast_transpose_tiles(a, layer, rows):
    _, n, c = a.shape
    return pl.pallas_call(
        _cast_transpose_kernel,
        grid=(n // rows,),
        in_specs=[pl.BlockSpec((None, rows, c), lambda i: (layer, i, 0))],
        out_specs=pl.BlockSpec((1, c, rows), lambda i: (i, 0, 0)),
        out_shape=jax.ShapeDtypeStruct((n // rows, c, rows), BF16),
        compiler_params=pltpu.CompilerParams(dimension_semantics=("arbitrary",), vmem_limit_bytes=VMEM_LIMIT),
        name="cast_transpose_tiles",
    )(a)


def _mixer_kernel(nb, tt, carry, offset,
                  u_ref, x_ref, ha_ref, hb_ref, hc_ref, h0_ref, hd_ref,
                  caw_ref, pw_ref, ps_ref, ccw_ref, ccb_ref, wa_ref, ba_ref, wi_ref, bi_ref, lam_ref,
                  cdw_ref, cdb_ref, lng_ref, lnb_ref, bg_ref,
                  wpa_ref, wpb_ref, wpc_ref, wpd_ref, wout_ref,
                  xo_ref, sa_ref, sb_ref, sc_ref, hl_ref, sd_ref,
                  ea_ref, eb_ref, ec_ref, ed_ref, h_ref, a_ref, bx_ref, hs_ref, yb_ref, gate_ref):
    r = tt * nb
    step = pl.program_id(0)
    first = (step == 0) if carry else None

    def load_hist():
        ea_ref[0:(CONV_A - 1) * nb, :] = ha_ref[...]
        eb_ref[0:POOL_BUF * nb, :] = hb_ref[...]
        ec_ref[0:(CONV_C - 1) * nb, :] = hc_ref[...]
        ed_ref[0:(CONV_D - 1) * nb, :] = hd_ref[...]
        h_ref[...] = h0_ref[...]

    if carry:
        pl.when(first)(load_hist)
    else:
        load_hist()

    z = u_ref[:, W_A:2 * W_A] * u_ref[:, 2 * W_A:3 * W_A]
    ea_ref[(CONV_A - 1) * nb:, :] = z
    y = caw_ref[0:1, :] * ea_ref[0:r, :]
    for k in range(1, CONV_A):
        y = y + caw_ref[k:k + 1, :] * ea_ref[k * nb:k * nb + r, :]
    ya = u_ref[:, 0:W_A] * y

    eb_ref[POOL_BUF * nb:, :] = u_ref[:, OFF_B:OFF_C]
    row = lax.broadcasted_iota(jnp.int32, (r, POOL_GROUP), 0)
    tpos = (row >> (nb.bit_length() - 1)) + (step * tt if carry else 0) + offset
    for g, w in enumerate(POOL_WINDOWS):
        sl = slice(g * POOL_GROUP, (g + 1) * POOL_GROUP)
        acc = eb_ref[POOL_BUF * nb:POOL_BUF * nb + r, sl]
        for k in range(1, w):
            acc = acc + eb_ref[(POOL_BUF - k) * nb:(POOL_BUF - k) * nb + r, sl]
        cnt = jnp.minimum(tpos + 1, w).astype(F32)
        p = acc / cnt - eb_ref[POOL_BUF * nb:POOL_BUF * nb + r, sl]
        yb_ref[:, sl] = _bdot(p, pw_ref[g]) * ps_ref[:, sl]

    ec_ref[(CONV_C - 1) * nb:, :] = u_ref[:, OFF_C:OFF_C + W_C]
    xconv = ccw_ref[0:1, :] * ec_ref[0:r, :]
    for k in range(1, CONV_C):
        xconv = xconv + ccw_ref[k:k + 1, :] * ec_ref[k * nb:k * nb + r, :]
    xconv = xconv + ccb_ref[...]
    lam = lam_ref[...]
    nsp = -LRU_C * (jnp.maximum(-lam, 0.0) + jnp.log(1.0 + jnp.exp(-jnp.abs(lam))))
    for hh in range(LRU_HEADS):
        sl = slice(hh * LRU_HEAD_DIM, (hh + 1) * LRU_HEAD_DIM)
        xh = xconv[:, sl]
        xhb = xh.astype(BF16)
        rg = _sigmoid(jnp.dot(xhb, wa_ref[hh], preferred_element_type=F32) + ba_ref[:, sl])
        ig = _sigmoid(jnp.dot(xhb, wi_ref[hh], preferred_element_type=F32) + bi_ref[:, sl])
        log_a = rg * nsp[:, sl]
        a = jnp.exp(log_a)
        a_ref[:, sl] = a
        bx_ref[:, sl] = jnp.sqrt(1.0 - a * a) * (ig * xh)

    if tt <= 8:
        for t in range(tt):
            h = a_ref[t * nb:(t + 1) * nb, :] * h_ref[...] + bx_ref[t * nb:(t + 1) * nb, :]
            h_ref[...] = h
            hs_ref[t * nb:(t + 1) * nb, :] = h
    else:
        def scan_body(t, h):
            rows = pl.ds(pl.multiple_of(t * nb, nb), nb)
            h = a_ref[rows, :] * h + bx_ref[rows, :]
            hs_ref[rows, :] = h
            return h
        h_ref[...] = lax.fori_loop(0, tt, scan_body, h_ref[...])
    yc = hs_ref[...] * _gelu(u_ref[:, OFF_C + W_C:OFF_D])

    ed_ref[(CONV_D - 1) * nb:, :] = u_ref[:, OFF_D:OFF_D + W_D] * _sigmoid(u_ref[:, OFF_D + W_D:OFF_G])
    c = cdw_ref[0:1, :] * ed_ref[0:r, :]
    for k in range(1, CONV_D):
        c = c + cdw_ref[k:k + 1, :] * ed_ref[k * nb:k * nb + r, :]
    c = c + cdb_ref[...]
    mu = jnp.mean(c, axis=-1, keepdims=True)
    cc = c - mu
    var = jnp.mean(cc * cc, axis=-1, keepdims=True)
    ln = cc * lax.rsqrt(var + EPS) * lng_ref[...] + lnb_ref[...]
    yd = ln * _sigmoid(ln)

    gate_ref[...] = _sigmoid(u_ref[:, OFF_G:] + bg_ref[...])
    m = gate_ref[:, 0:D_MODEL] * _bdot(ya, wpa_ref[...])
    m = m + gate_ref[:, D_MODEL:2 * D_MODEL] * _bdot(yb_ref[...], wpb_ref[...])
    m = m + gate_ref[:, 2 * D_MODEL:3 * D_MODEL] * _bdot(yc, wpc_ref[...])
    m = m + gate_ref[:, 3 * D_MODEL:] * _bdot(yd, wpd_ref[...])
    xo_ref[...] = x_ref[...] + _bdot(m, wout_ref[...])

    def store_state():
        sa_ref[...] = ea_ref[r:r + (CONV_A - 1) * nb, :]
        sb_ref[...] = eb_ref[r:r + POOL_BUF * nb, :]
        sc_ref[...] = ec_ref[r:r + (CONV_C - 1) * nb, :]
        sd_ref[...] = ed_ref[r:r + (CONV_D - 1) * nb, :]
        hl_ref[...] = h_ref[...]

    if carry:
        pl.when(step == pl.num_programs(0) - 1)(store_state)
        for e_ref, hist in ((ea_ref, CONV_A - 1), (eb_ref, POOL_BUF), (ec_ref, CONV_C - 1), (ed_ref, CONV_D - 1)):
            total = hist * nb
            for s in range(0, total, r):
                n = min(r, total - s)
                e_ref[s:s + n, :] = e_ref[r + s:r + s + n, :]
    else:
        store_state()


def _mixer(u, x, hists, lp, nb, tt, carry, offset):
    n = x.shape[0]
    r = nb * tt
    steps = n // r
    ha, hb, hc, h0, hd = hists
    widths = (W_A, W_B, W_C, W_D)
    hist_rows = ((CONV_A - 1) * nb, POOL_BUF * nb, (CONV_C - 1) * nb, (CONV_D - 1) * nb)

    def chunk_map(i):
        return (0, 0) if carry else (i, 0)

    def const2(i):
        return (0, 0)

    def const3(i):
        return (0, 0, 0)

    def full(a):
        return pl.BlockSpec(a.shape, const2 if a.ndim == 2 else const3)

    params = [lp['conv_a_w'], lp['pool_w'], lp['pool_scale'], lp['conv_c_w'], lp['conv_c_b'],
              lp['lru_wa'], lp['lru_ba'], lp['lru_wi'], lp['lru_bi'], lp['lru_lambda'],
              lp['conv_d_w'], lp['conv_d_b'], lp['ln_d_g'], lp['ln_d_b'], lp['b_gate'],
              lp['w_pa'], lp['w_pb'], lp['w_pc'], lp['w_pd'], lp['w_out']]
    in_specs = [
        pl.BlockSpec((r, IN_TOTAL), lambda i: (i, 0)),
        pl.BlockSpec((r, D_MODEL), lambda i: (i, 0)),
        pl.BlockSpec((hist_rows[0], W_A), chunk_map),
        pl.BlockSpec((hist_rows[1], W_B), chunk_map),
        pl.BlockSpec((hist_rows[2], W_C), chunk_map),
        pl.BlockSpec((nb, W_C), chunk_map),
        pl.BlockSpec((hist_rows[3], W_D), chunk_map),
    ] + [full(p) for p in params]
    n_state = 1 if carry else steps
    out_shape = (
        jax.ShapeDtypeStruct((n, D_MODEL), F32),
        jax.ShapeDtypeStruct((n_state * hist_rows[0], W_A), F32),
        jax.ShapeDtypeStruct((n_state * hist_rows[1], W_B), F32),
        jax.ShapeDtypeStruct((n_state * hist_rows[2], W_C), F32),
        jax.ShapeDtypeStruct((n_state * nb, W_C), F32),
        jax.ShapeDtypeStruct((n_state * hist_rows[3], W_D), F32),
    )
    out_specs = (
        pl.BlockSpec((r, D_MODEL), lambda i: (i, 0)),
        pl.BlockSpec((hist_rows[0], W_A), chunk_map),
        pl.BlockSpec((hist_rows[1], W_B), chunk_map),
        pl.BlockSpec((hist_rows[2], W_C), chunk_map),
        pl.BlockSpec((nb, W_C), chunk_map),
        pl.BlockSpec((hist_rows[3], W_D), chunk_map),
    )
    scratch = [pltpu.VMEM((hr + r, w), F32) for hr, w in zip(hist_rows, widths)]
    scratch += [
        pltpu.VMEM((nb, W_C), F32),
        pltpu.VMEM((r, W_C), F32),
        pltpu.VMEM((r, W_C), F32),
        pltpu.VMEM((r, W_C), F32),
        pltpu.VMEM((r, W_B), F32),
        pltpu.VMEM((r, 4 * D_MODEL), F32),
    ]
    return pl.pallas_call(
        functools.partial(_mixer_kernel, nb, tt, carry, offset),
        grid=(steps,),
        in_specs=in_specs,
        out_specs=out_specs,
        out_shape=out_shape,
        scratch_shapes=scratch,
        compiler_params=pltpu.CompilerParams(
            dimension_semantics=("arbitrary",), vmem_limit_bytes=VMEM_LIMIT),
        name="mixer_carry" if carry else "mixer_batch",
    )(u, x, ha, hb, hc, h0, hd, *params)


N_TOP = PEER_TOPK + 1
SV_ROWS = 24


PRO_LANES = 2 * LANES
SUBLANES = 8
MXU_COLS = 256


def _batcher_pairs(n):
    pairs = []

    def merge(lo, hi, r):
        step = r * 2
        if step < hi - lo:
            merge(lo, hi, step)
            merge(lo + r, hi, step)
            pairs.extend((i, i + r) for i in range(lo + r, hi - r, step))
        else:
            pairs.append((lo, lo + r))

    def sort(lo, hi):
        if hi - lo >= 1:
            mid = lo + (hi - lo) // 2
            sort(lo, mid)
            sort(mid + 1, hi)
            merge(lo, hi, 1)

    sort(0, n - 1)
    return pairs


def _sort_desc(ws):
    n = 1
    while n < len(ws):
        n *= 2
    ws = list(ws) + [None] * (n - len(ws))
    for i, j in _batcher_pairs(n):
        a, b = ws[i], ws[j]
        if b is None:
            continue
        if a is None:
            ws[i], ws[j] = b, None
        else:
            ws[i], ws[j] = jnp.maximum(a, b), jnp.minimum(a, b)
    return [w for w in ws if w is not None]


def _top_rows(ws, n):
    out = []
    for k in range(n):
        m = jnp.max(ws[0], axis=0, keepdims=True)
        out.append(m)
        remaining = n - 1 - k
        if remaining == 0:
            break
        hit = ws[0] >= m
        ws = [jnp.where(hit, ws[i + 1] if i + 1 < len(ws) else NEG_INF, ws[i])
              for i in range(min(len(ws), remaining))]
    return out


def _peer_kernel(tb, et, ts, final,
                 x_ref, g_ref, wqt_ref, keys_ref, u_ref, vt_ref, vtp_ref, fg_ref,
                 o_ref,
                 xnt_ref, s_ref, sv_ref, a_ref, b_ref, e2_ref, ht_ref, ct_ref, acc_ref):
    npl = tb // PRO_LANES
    rows_per_tile = et // N_KEYS
    j = pl.program_id(1)

    @pl.when(j == 0)
    def _():
        xn = _rms(x_ref[...], g_ref[...])
        for pt in range(npl):
            xnt_ref[pt] = xn[pt * PRO_LANES:(pt + 1) * PRO_LANES, :].T.astype(BF16)

        def scores_into(h, pt):
            rows = pl.ds(pl.multiple_of(h * 2 * KEY_DIM, 2 * KEY_DIM), 2 * KEY_DIM)
            q = jnp.dot(wqt_ref[rows, :], xnt_ref[pt], preferred_element_type=F32).astype(BF16)
            for half in range(2):
                s_ref[2 * h + half, pt] = jnp.dot(keys_ref[2 * h + half], q[half * KEY_DIM:(half + 1) * KEY_DIM, :],
                                                  preferred_element_type=F32)

        n_trips = PEER_HEADS * npl
        scores_into(0, 0)

        def select_body(idx, carry):
            h = idx // npl
            pt = idx % npl
            for hc in (2 * h, 2 * h + 1):
                ws = _sort_desc([s_ref[hc, pt, i * SUBLANES:(i + 1) * SUBLANES, :]
                                 for i in range(N_KEYS // SUBLANES)])
                rows = _top_rows(ws, N_TOP)
                sv_ref[hc, pt] = jnp.full((SV_ROWS, PRO_LANES), NEG_INF, F32)
                for k in range(N_TOP):
                    sv_ref[hc, pt, k:k + 1, :] = rows[k]
            sv1 = sv_ref[2 * h, pt]
            sv2 = sv_ref[2 * h + 1, pt]
            pieces = [sv1[0:1, :] + sv2[i * SUBLANES:(i + 1) * SUBLANES, :] for i in range(SV_ROWS // SUBLANES)]
            pieces += [sv1[a:a + 1, :] + sv2[0:SUBLANES, :] for a in range(1, SUBLANES)]
            pieces += [sv1[i * SUBLANES:(i + 1) * SUBLANES, :] + sv2[0:1, :] for i in range(1, SV_ROWS // SUBLANES)]
            best = _top_rows(_sort_desc(pieces), N_TOP)
            zsum = jnp.ones_like(best[0])
            for k in range(1, PEER_TOPK):
                zsum = zsum + jnp.exp(best[k] - best[0])
            thr = 0.5 * (best[PEER_TOPK - 1] + best[PEER_TOPK])
            s1 = s_ref[2 * h, pt]
            s2 = s_ref[2 * h + 1, pt]
            a_ref[h, pt] = jnp.exp(s1 - sv1[0:1, :]) / zsum
            b_ref[h, pt] = thr - s1
            e2_ref[h, pt] = jnp.exp(s2 - sv2[0:1, :])
            nxt = jnp.minimum(idx + 1, n_trips - 1)
            scores_into(nxt // npl, nxt % npl)
            return carry

        lax.fori_loop(0, n_trips, select_body, 0)
        acc_ref[...] = jnp.zeros_like(acc_ref)
        ct_ref[(ts - 1) * et:ts * et, tb - MXU_COLS:tb] = jnp.zeros((et, MXU_COLS), BF16)

    pieces = [(t, c) for t in range(ts) for c in range(tb // MXU_COLS)]
    units = [(t, c, rr) for (t, c) in pieces for rr in range(rows_per_tile)]
    out_rows = D_MODEL // rows_per_tile

    def pre_activations(t, c, rr):
        er = slice(t * et + rr * N_KEYS, t * et + (rr + 1) * N_KEYS)
        cols = slice(c * MXU_COLS, (c + 1) * MXU_COLS)
        ht_ref[er, cols] = jnp.dot(u_ref[er, :], xnt_ref[c], preferred_element_type=F32)

    def output_term(t, c, m, v_ref=None):
        rows = slice(t * et, (t + 1) * et)
        cols = slice(c * MXU_COLS, (c + 1) * MXU_COLS)
        mr = slice(m * out_rows, (m + 1) * out_rows)
        v = vt_ref[t, mr, :] if v_ref is None else v_ref[0, mr, :]
        acc_ref[mr, cols] += jnp.dot(v, ct_ref[rows, cols], preferred_element_type=F32)

    def coefficient_tile(t, c, rr, l):
        i1 = (j * ts + t) * rows_per_tile + rr
        lt = c * (MXU_COLS // LANES) + l
        pt = lt // 2
        ln = slice((lt % 2) * LANES, (lt % 2 + 1) * LANES)
        g = jnp.zeros((N_KEYS, LANES), F32)
        for h in range(PEER_HEADS):
            a_row = a_ref[h, pt, pl.ds(i1, 1), :][:, ln]
            b_row = b_ref[h, pt, pl.ds(i1, 1), :][:, ln]
            g = g + jnp.where(s_ref[2 * h + 1, pt, :, ln] >= b_row, e2_ref[h, pt, :, ln], 0.0) * a_row
        er = slice(t * et + rr * N_KEYS, t * et + (rr + 1) * N_KEYS)
        tl = slice(lt * LANES, (lt + 1) * LANES)
        ct_ref[er, tl] = (g * _gelu_sigmoid_form(ht_ref[er, tl])).astype(BF16)

    pre_activations(*units[0])
    for n, (t, c, rr) in enumerate(units):
        if n + 1 < len(units):
            pre_activations(*units[n + 1])
        coefficient_tile(t, c, rr, 0)
        if n >= rows_per_tile:
            pt_, pc_, _ = units[n - rows_per_tile]
            output_term(pt_, pc_, rr)
        else:
            output_term(*pieces[-1], rr, v_ref=vtp_ref)
        coefficient_tile(t, c, rr, 1)

    @pl.when(j == pl.num_programs(1) - 1)
    def _():
        for m in range(rows_per_tile):
            output_term(*pieces[-1], m)
        y = x_ref[...] + acc_ref[...].T
        if final:
            y = _rms(y, fg_ref[...])
        o_ref[...] = y


def _peer(x, g, wqt, keys, u_tab, vt_tiles, final_g, final, tb, et, ts):
    n = x.shape[0]
    npl = tb // PRO_LANES
    return pl.pallas_call(
        functools.partial(_peer_kernel, tb, et, ts, final),
        grid=(n // tb, N_EXPERTS // (ts * et)),
        in_specs=[
            pl.BlockSpec((tb, D_MODEL), lambda i, j: (i, 0)),
            pl.BlockSpec((1, D_MODEL), lambda i, j: (0, 0)),
            pl.BlockSpec(wqt.shape, lambda i, j: (0, 0)),
            pl.BlockSpec(keys.shape, lambda i, j: (0, 0, 0)),
            pl.BlockSpec((ts * et, D_MODEL), lambda i, j: (j, 0)),
            pl.BlockSpec((ts, D_MODEL, et), lambda i, j: (j, 0, 0)),
            pl.BlockSpec((1, D_MODEL, et), lambda i, j: (jnp.maximum(j * ts - 1, 0), 0, 0)),
            pl.BlockSpec((1, D_MODEL), lambda i, j: (0, 0)),
        ],
        out_specs=pl.BlockSpec((tb, D_MODEL), lambda i, j: (i, 0)),
        out_shape=jax.ShapeDtypeStruct((n, D_MODEL), F32),
        scratch_shapes=[
            pltpu.VMEM((npl, D_MODEL, PRO_LANES), BF16),
            pltpu.VMEM((2 * PEER_HEADS, npl, N_KEYS, PRO_LANES), F32),
            pltpu.VMEM((2 * PEER_HEADS, npl, SV_ROWS, PRO_LANES), F32),
            pltpu.VMEM((PEER_HEADS, npl, N_KEYS, PRO_LANES), F32),
            pltpu.VMEM((PEER_HEADS, npl, N_KEYS, PRO_LANES), F32),
            pltpu.VMEM((PEER_HEADS, npl, N_KEYS, PRO_LANES), F32),
            pltpu.VMEM((ts * et, tb), F32),
            pltpu.VMEM((ts * et, tb), BF16),
            pltpu.VMEM((D_MODEL, tb), F32),
        ],
        compiler_params=pltpu.CompilerParams(
            dimension_semantics=("arbitrary", "arbitrary"), vmem_limit_bytes=VMEM_LIMIT),
        name="peer_final" if final else "peer",
    )(x, g, wqt, keys, u_tab, vt_tiles, vt_tiles, final_g)


PROMPT_NB = 8
PROMPT_TT = 32
SAMPLE_NB = 32
CAST_ROWS_W_IN = 256
CAST_ROWS_TABLE = 2048
INPROJ_TM = 1024
INPROJ_TN = 1536
PEER_TB = 512
PEER_ET = 512
PEER_TS = 4


def _to_chunks(a, nb):
    b, t, c = a.shape
    return a.reshape(b // nb, nb, t, c).transpose(0, 2, 1, 3).reshape(b * t, c)


def _from_chunks(a, nb, b, t):
    c = a.shape[-1]
    return a.reshape(b // nb, t, nb, c).transpose(0, 2, 1, 3).reshape(b, t, c)


def kernel(x_prompt, x_sample, state_conv_a, state_pool, state_conv_c, state_lru_h, state_conv_d, norm1_g, w_in, b_gate, conv_a_w, pool_w, pool_scale, conv_c_w, conv_c_b, lru_wa, lru_ba, lru_wi, lru_bi, lru_lambda, conv_d_w, conv_d_b, ln_d_g, ln_d_b, w_pa, w_pb, w_pc, w_pd, w_out, norm2_g, peer_wq, peer_keys, peer_u, peer_v, final_g):
    bp, tp, _ = x_prompt.shape
    bs, ts, _ = x_sample.shape
    xp = _to_chunks(x_prompt, PROMPT_NB)
    xs = _to_chunks(x_sample, SAMPLE_NB)
    fg = final_g.reshape(1, D_MODEL)

    outs_p = [[] for _ in range(5)]
    outs_s = [[] for _ in range(5)]
    for l in range(DEPTH):
        lp = {
            'conv_a_w': conv_a_w[l], 'pool_w': pool_w[l].astype(BF16), 'pool_scale': pool_scale[l].reshape(1, W_B),
            'conv_c_w': conv_c_w[l], 'conv_c_b': conv_c_b[l].reshape(1, W_C),
            'lru_wa': lru_wa[l].astype(BF16), 'lru_ba': lru_ba[l].reshape(1, W_C),
            'lru_wi': lru_wi[l].astype(BF16), 'lru_bi': lru_bi[l].reshape(1, W_C),
            'lru_lambda': lru_lambda[l].reshape(1, W_C),
            'conv_d_w': conv_d_w[l], 'conv_d_b': conv_d_b[l].reshape(1, W_D),
            'ln_d_g': ln_d_g[l].reshape(1, W_D), 'ln_d_b': ln_d_b[l].reshape(1, W_D),
            'b_gate': b_gate[l].reshape(1, 4 * D_MODEL),
            'w_pa': w_pa[l].astype(BF16), 'w_pb': w_pb[l].astype(BF16), 'w_pc': w_pc[l].astype(BF16),
            'w_pd': w_pd[l].astype(BF16), 'w_out': w_out[l].astype(BF16),
        }
        g1 = norm1_g[l].reshape(1, D_MODEL)
        g2 = norm2_g[l].reshape(1, D_MODEL)
        w_in_b = _cast_bf16(w_in, l, CAST_ROWS_W_IN)
        wqt = peer_wq[l].T.astype(BF16)
        keys = peer_keys[l].reshape(2 * PEER_HEADS, N_KEYS, KEY_DIM).astype(BF16)
        u_tab = _cast_bf16(peer_u, l, CAST_ROWS_TABLE)
        vt_tab = _cast_transpose_tiles(peer_v, l, PEER_ET)
        final = l == DEPTH - 1

        zero_hist = (jnp.zeros(((CONV_A - 1) * bp, W_A), F32), jnp.zeros((POOL_BUF * bp, W_B), F32),
                     jnp.zeros(((CONV_C - 1) * bp, W_C), F32), jnp.zeros((bp, W_C), F32),
                     jnp.zeros(((CONV_D - 1) * bp, W_D), F32))
        u = _inproj(xp, g1, w_in_b, INPROJ_TM, INPROJ_TN)
        xp, *st = _mixer(u, xp, zero_hist, lp, PROMPT_NB, PROMPT_TT, True, 0)
        for acc, s in zip(outs_p, st):
            acc.append(s)
        xp = _peer(xp, g2, wqt, keys, u_tab, vt_tab, fg, final, PEER_TB, PEER_ET, PEER_TS)

        hist = (_to_chunks(state_conv_a[l], SAMPLE_NB), _to_chunks(state_pool[l], SAMPLE_NB),
                _to_chunks(state_conv_c[l], SAMPLE_NB), state_lru_h[l], _to_chunks(state_conv_d[l], SAMPLE_NB))
        u = _inproj(xs, g1, w_in_b, min(INPROJ_TM, xs.shape[0]), INPROJ_TN)
        xs, *st = _mixer(u, xs, hist, lp, SAMPLE_NB, ts, False, PAST_LEN)
        for acc, s in zip(outs_s, st):
            acc.append(s)
        xs = _peer(xs, g2, wqt, keys, u_tab, vt_tab, fg, final, PEER_TB, PEER_ET, PEER_TS)

    y_prompt = _from_chunks(xp, PROMPT_NB, bp, tp)
    y_sample = _from_chunks(xs, SAMPLE_NB, bs, ts)

    def states(outs, nb, b):
        res = []
        for idx, hist_len in ((0, CONV_A - 1), (1, POOL_BUF), (2, CONV_C - 1), (3, None), (4, CONV_D - 1)):
            if hist_len is None:
                res.append(jnp.stack(outs[idx]))
            else:
                res.append(jnp.stack([_from_chunks(s, nb, b, hist_len) for s in outs[idx]]))
        return res

    ca_p, pool_p, cc_p, h_p, cd_p = states(outs_p, PROMPT_NB, bp)
    ca_s, pool_s, cc_s, h_s, cd_s = states(outs_s, SAMPLE_NB, bs)
    return (y_prompt, y_sample, ca_p, ca_s, pool_p, pool_s, cc_p, cc_s, h_p, h_s, cd_p, cd_s)
```

```python
import functools

import jax
import jax.numpy as jnp
from jax import lax
from jax.experimental import pallas as pl
from jax.experimental.pallas import tpu as pltpu

F32 = jnp.float32
BF16 = jnp.bfloat16

D_MODEL = 1024
DEPTH = 2
PAST_LEN = 16384
W_A = 512
CONV_A = 3
W_B = 512
POOL_WINDOWS = (2, 4, 8, 16)
POOL_GROUP = 128
POOL_BUF = 15
W_C = 1024
LRU_HEADS = 8
LRU_HEAD_DIM = 128
CONV_C = 4
LRU_C = 8.0
W_D = 512
CONV_D = 31
OFF_B = 3 * W_A
OFF_C = OFF_B + W_B
OFF_D = OFF_C + 2 * W_C
OFF_G = OFF_D + 2 * W_D
IN_TOTAL = OFF_G + 4 * D_MODEL
PEER_HEADS = 8
N_KEYS = 128
N_EXPERTS = N_KEYS * N_KEYS
KEY_DIM = 128
PEER_TOPK = 16
EPS = 1e-6

LANES = 128
VMEM_LIMIT = 56 * 1024 * 1024

GELU_C0 = 0.7978845608028654
GELU_C1 = 0.044715
LOG2_E = 1.4426950408889634
NEG_INF = float("-inf")


def _sigmoid(x):
    return 1.0 / (1.0 + jnp.exp2(x * (-LOG2_E)))


def _gelu_sigmoid_form(x):
    q = x * (x * x * (-2.0 * GELU_C0 * GELU_C1 * LOG2_E) + (-2.0 * GELU_C0 * LOG2_E))
    return x / (1.0 + jnp.exp2(q))


def _rms(x, g):
    ms = jnp.mean(x * x, axis=-1, keepdims=True)
    return x * lax.rsqrt(ms + EPS) * g


def _bdot(a, b):
    return jnp.dot(a.astype(BF16), b.astype(BF16), preferred_element_type=F32)


def _inproj_kernel(x_ref, g_ref, w_ref, u_ref, xn_ref):
    @pl.when(pl.program_id(1) == 0)
    def _():
        xn_ref[...] = _rms(x_ref[...], g_ref[...]).astype(BF16)

    u_ref[...] = jnp.dot(xn_ref[...], w_ref[...], preferred_element_type=F32)


def _inproj(x, g, w_bf16, tm, tn):
    n = x.shape[0]
    assert n % tm == 0 and IN_TOTAL % tn == 0 and tn % LANES == 0
    return pl.pallas_call(
        _inproj_kernel,
        grid=(n // tm, IN_TOTAL // tn),
        in_specs=[
            pl.BlockSpec((tm, D_MODEL), lambda i, j: (i, 0)),
            pl.BlockSpec((1, D_MODEL), lambda i, j: (0, 0)),
            pl.BlockSpec((D_MODEL, tn), lambda i, j: (0, j)),
        ],
        out_specs=pl.BlockSpec((tm, tn), lambda i, j: (i, j)),
        out_shape=jax.ShapeDtypeStruct((n, IN_TOTAL), F32),
        scratch_shapes=[pltpu.VMEM((tm, D_MODEL), BF16)],
        compiler_params=pltpu.CompilerParams(
            dimension_semantics=("arbitrary", "arbitrary"), vmem_limit_bytes=VMEM_LIMIT),
        name="inproj",
    )(x, g, w_bf16)


def _cast_kernel(x_ref, o_ref):
    o_ref[...] = x_ref[...].astype(BF16)


def _cast_bf16(a, layer, rows):
    _, n, c = a.shape
    return pl.pallas_call(
        _cast_kernel,
        grid=(n // rows,),
        in_specs=[pl.BlockSpec((None, rows, c), lambda i: (layer, i, 0))],
        out_specs=pl.BlockSpec((rows, c), lambda i: (i, 0)),
        out_shape=jax.ShapeDtypeStruct((n, c), BF16),
        compiler_params=pltpu.CompilerParams(dimension_semantics=("arbitrary",), vmem_limit_bytes=VMEM_LIMIT),
        name="cast_bf16",
    )(a)


def _cast_transpose_kernel(x_ref, o_ref):
    o_ref[0] = x_ref[...].T.astype(BF16)


def _cast_transpose_tiles(a, layer, rows):
    _, n, c = a.shape
    return pl.pallas_call(
        _cast_transpose_kernel,
        grid=(n // rows,),
        in_specs=[pl.BlockSpec((None, rows, c), lambda i: (layer, i, 0))],
        out_specs=pl.BlockSpec((1, c, rows), lambda i: (i, 0, 0)),
        out_shape=jax.ShapeDtypeStruct((n // rows, c, rows), BF16),
        compiler_params=pltpu.CompilerParams(dimension_semantics=("arbitrary",), vmem_limit_bytes=VMEM_LIMIT),
        name="cast_transpose_tiles",
    )(a)


def _mixer_kernel(nb, tt, carry, offset,
                  u_ref, x_ref, ha_ref, hb_ref, hc_ref, h0_ref, hd_ref,
                  caw_ref, pw_ref, ps_ref, ccw_ref, ccb_ref, wa_ref, ba_ref, wi_ref, bi_ref, lam_ref,
                  cdw_ref, cdb_ref, lng_ref, lnb_ref, bg_ref,
                  wpa_ref, wpb_ref, wpc_ref, wpd_ref, wout_ref,
                  xo_ref, sa_ref, sb_ref, sc_ref, hl_ref, sd_ref,
                  ea_ref, eb_ref, ec_ref, ed_ref, h_ref, a_ref, bx_ref, hs_ref, yb_ref, gate_ref):
    r = tt * nb
    step = pl.program_id(0)
    first = (step == 0) if carry else None

    def load_hist():
        ea_ref[0:(CONV_A - 1) * nb, :] = ha_ref[...]
        eb_ref[0:POOL_BUF * nb, :] = hb_ref[...]
        ec_ref[0:(CONV_C - 1) * nb, :] = hc_ref[...]
        ed_ref[0:(CONV_D - 1) * nb, :] = hd_ref[...]
        h_ref[...] = h0_ref[...]

    if carry:
        pl.when(first)(load_hist)
    else:
        load_hist()

    z = u_ref[:, W_A:2 * W_A] * u_ref[:, 2 * W_A:3 * W_A]
    ea_ref[(CONV_A - 1) * nb:, :] = z
    y = caw_ref[0:1, :] * ea_ref[0:r, :]
    for k in range(1, CONV_A):
        y = y + caw_ref[k:k + 1, :] * ea_ref[k * nb:k * nb + r, :]
    ya = u_ref[:, 0:W_A] * y

    eb_ref[POOL_BUF * nb:, :] = u_ref[:, OFF_B:OFF_C]
    row = lax.broadcasted_iota(jnp.int32, (r, POOL_GROUP), 0)
    tpos = (row >> (nb.bit_length() - 1)) + (step * tt if carry else 0) + offset
    for g, w in enumerate(POOL_WINDOWS):
        sl = slice(g * POOL_GROUP, (g + 1) * POOL_GROUP)
        acc = eb_ref[POOL_BUF * nb:POOL_BUF * nb + r, sl]
        for k in range(1, w):
            acc = acc + eb_ref[(POOL_BUF - k) * nb:(POOL_BUF - k) * nb + r, sl]
        cnt = jnp.minimum(tpos + 1, w).astype(F32)
        p = acc / cnt - eb_ref[POOL_BUF * nb:POOL_BUF * nb + r, sl]
        yb_ref[:, sl] = _bdot(p, pw_ref[g]) * ps_ref[:, sl]

    ec_ref[(CONV_C - 1) * nb:, :] = u_ref[:, OFF_C:OFF_C + W_C]
    xconv = ccw_ref[0:1, :] * ec_ref[0:r, :]
    for k in range(1, CONV_C):
        xconv = xconv + ccw_ref[k:k + 1, :] * ec_ref[k * nb:k * nb + r, :]
    xconv = xconv + ccb_ref[...]
    lam = lam_ref[...]
    nsp = -LRU_C * (jnp.maximum(-lam, 0.0) + jnp.log(1.0 + jnp.exp(-jnp.abs(lam))))
    for hh in range(LRU_HEADS):
        sl = slice(hh * LRU_HEAD_DIM, (hh + 1) * LRU_HEAD_DIM)
        xh = xconv[:, sl]
        xhb = xh.astype(BF16)
        rg = _sigmoid(jnp.dot(xhb, wa_ref[hh], preferred_element_type=F32) + ba_ref[:, sl])
        ig = _sigmoid(jnp.dot(xhb, wi_ref[hh], preferred_element_type=F32) + bi_ref[:, sl])
        log_a = rg * nsp[:, sl]
        a = jnp.exp(log_a)
        a_ref[:, sl] = a
        bx_ref[:, sl] = jnp.sqrt(1.0 - a * a) * (ig * xh)

    if tt <= 8:
        for t in range(tt):
            h = a_ref[t * nb:(t + 1) * nb, :] * h_ref[...] + bx_ref[t * nb:(t + 1) * nb, :]
            h_ref[...] = h
            hs_ref[t * nb:(t + 1) * nb, :] = h
    else:
        def scan_body(t, h):
            rows = pl.ds(pl.multiple_of(t * nb, nb), nb)
            h = a_ref[rows, :] * h + bx_ref[rows, :]
            hs_ref[rows, :] = h
            return h
        h_ref[...] = lax.fori_loop(0, tt, scan_body, h_ref[...])
    yc = hs_ref[...] * _gelu_sigmoid_form(u_ref[:, OFF_C + W_C:OFF_D])

    ed_ref[(CONV_D - 1) * nb:, :] = u_ref[:, OFF_D:OFF_D + W_D] * _sigmoid(u_ref[:, OFF_D + W_D:OFF_G])
    c = cdw_ref[0:1, :] * ed_ref[0:r, :]
    for k in range(1, CONV_D):
        c = c + cdw_ref[k:k + 1, :] * ed_ref[k * nb:k * nb + r, :]
    c = c + cdb_ref[...]
    mu = jnp.mean(c, axis=-1, keepdims=True)
    cc = c - mu
    var = jnp.mean(cc * cc, axis=-1, keepdims=True)
    ln = cc * lax.rsqrt(var + EPS) * lng_ref[...] + lnb_ref[...]
    yd = ln * _sigmoid(ln)

    gate_ref[...] = _sigmoid(u_ref[:, OFF_G:] + bg_ref[...])
    m = gate_ref[:, 0:D_MODEL] * _bdot(ya, wpa_ref[...])
    m = m + gate_ref[:, D_MODEL:2 * D_MODEL] * _bdot(yb_ref[...], wpb_ref[...])
    m = m + gate_ref[:, 2 * D_MODEL:3 * D_MODEL] * _bdot(yc, wpc_ref[...])
    m = m + gate_ref[:, 3 * D_MODEL:] * _bdot(yd, wpd_ref[...])
    xo_ref[...] = x_ref[...] + _bdot(m, wout_ref[...])

    def store_state():
        sa_ref[...] = ea_ref[r:r + (CONV_A - 1) * nb, :]
        sb_ref[...] = eb_ref[r:r + POOL_BUF * nb, :]
        sc_ref[...] = ec_ref[r:r + (CONV_C - 1) * nb, :]
        sd_ref[...] = ed_ref[r:r + (CONV_D - 1) * nb, :]
        hl_ref[...] = h_ref[...]

    if carry:
        pl.when(step == pl.num_programs(0) - 1)(store_state)
        for e_ref, hist in ((ea_ref, CONV_A - 1), (eb_ref, POOL_BUF), (ec_ref, CONV_C - 1), (ed_ref, CONV_D - 1)):
            total = hist * nb
            for s in range(0, total, r):
                n = min(r, total - s)
                e_ref[s:s + n, :] = e_ref[r + s:r + s + n, :]
    else:
        store_state()


def _mixer(u, x, hists, lp, nb, tt, carry, offset):
    n = x.shape[0]
    r = nb * tt
    steps = n // r
    assert n % r == 0 and nb % 8 == 0 and nb & (nb - 1) == 0
    ha, hb, hc, h0, hd = hists
    widths = (W_A, W_B, W_C, W_D)
    hist_rows = ((CONV_A - 1) * nb, POOL_BUF * nb, (CONV_C - 1) * nb, (CONV_D - 1) * nb)

    def chunk_map(i):
        return (0, 0) if carry else (i, 0)

    def const2(i):
        return (0, 0)

    def const3(i):
        return (0, 0, 0)

    def full(a):
        return pl.BlockSpec(a.shape, const2 if a.ndim == 2 else const3)

    params = [lp['conv_a_w'], lp['pool_w'], lp['pool_scale'], lp['conv_c_w'], lp['conv_c_b'],
              lp['lru_wa'], lp['lru_ba'], lp['lru_wi'], lp['lru_bi'], lp['lru_lambda'],
              lp['conv_d_w'], lp['conv_d_b'], lp['ln_d_g'], lp['ln_d_b'], lp['b_gate'],
              lp['w_pa'], lp['w_pb'], lp['w_pc'], lp['w_pd'], lp['w_out']]
    in_specs = [
        pl.BlockSpec((r, IN_TOTAL), lambda i: (i, 0)),
        pl.BlockSpec((r, D_MODEL), lambda i: (i, 0)),
        pl.BlockSpec((hist_rows[0], W_A), chunk_map),
        pl.BlockSpec((hist_rows[1], W_B), chunk_map),
        pl.BlockSpec((hist_rows[2], W_C), chunk_map),
        pl.BlockSpec((nb, W_C), chunk_map),
        pl.BlockSpec((hist_rows[3], W_D), chunk_map),
    ] + [full(p) for p in params]
    n_state = 1 if carry else steps
    out_shape = (
        jax.ShapeDtypeStruct((n, D_MODEL), F32),
        jax.ShapeDtypeStruct((n_state * hist_rows[0], W_A), F32),
        jax.ShapeDtypeStruct((n_state * hist_rows[1], W_B), F32),
        jax.ShapeDtypeStruct((n_state * hist_rows[2], W_C), F32),
        jax.ShapeDtypeStruct((n_state * nb, W_C), F32),
        jax.ShapeDtypeStruct((n_state * hist_rows[3], W_D), F32),
    )
    out_specs = (
        pl.BlockSpec((r, D_MODEL), lambda i: (i, 0)),
        pl.BlockSpec((hist_rows[0], W_A), chunk_map),
        pl.BlockSpec((hist_rows[1], W_B), chunk_map),
        pl.BlockSpec((hist_rows[2], W_C), chunk_map),
        pl.BlockSpec((nb, W_C), chunk_map),
        pl.BlockSpec((hist_rows[3], W_D), chunk_map),
    )
    scratch = [pltpu.VMEM((hr + r, w), F32) for hr, w in zip(hist_rows, widths)]
    scratch += [
        pltpu.VMEM((nb, W_C), F32),
        pltpu.VMEM((r, W_C), F32),
        pltpu.VMEM((r, W_C), F32),
        pltpu.VMEM((r, W_C), F32),
        pltpu.VMEM((r, W_B), F32),
        pltpu.VMEM((r, 4 * D_MODEL), F32),
    ]
    return pl.pallas_call(
        functools.partial(_mixer_kernel, nb, tt, carry, offset),
        grid=(steps,),
        in_specs=in_specs,
        out_specs=out_specs,
        out_shape=out_shape,
        scratch_shapes=scratch,
        compiler_params=pltpu.CompilerParams(
            dimension_semantics=("arbitrary",), vmem_limit_bytes=VMEM_LIMIT),
        name="mixer_carry" if carry else "mixer_batch",
    )(u, x, ha, hb, hc, h0, hd, *params)


N_TOP = PEER_TOPK + 1
SV_ROWS = 24


PRO_LANES = 2 * LANES
SUBLANES = 8
MXU_COLS = 256


def _batcher_pairs(n):
    pairs = []

    def merge(lo, hi, r):
        step = r * 2
        if step < hi - lo:
            merge(lo, hi, step)
            merge(lo + r, hi, step)
            pairs.extend((i, i + r) for i in range(lo + r, hi - r, step))
        else:
            pairs.append((lo, lo + r))

    def sort(lo, hi):
        if hi - lo >= 1:
            mid = lo + (hi - lo) // 2
            sort(lo, mid)
            sort(mid + 1, hi)
            merge(lo, hi, 1)

    sort(0, n - 1)
    return pairs


def _sort_desc(ws):
    n = 1
    while n < len(ws):
        n *= 2
    ws = list(ws) + [None] * (n - len(ws))
    for i, j in _batcher_pairs(n):
        a, b = ws[i], ws[j]
        if b is None:
            continue
        if a is None:
            ws[i], ws[j] = b, None
        else:
            ws[i], ws[j] = jnp.maximum(a, b), jnp.minimum(a, b)
    return [w for w in ws if w is not None]


def _top_rows(ws, n):
    out = []
    for k in range(n):
        m = jnp.max(ws[0], axis=0, keepdims=True)
        out.append(m)
        remaining = n - 1 - k
        if remaining == 0:
            break
        hit = ws[0] >= m
        ws = [jnp.where(hit, ws[i + 1] if i + 1 < len(ws) else NEG_INF, ws[i])
              for i in range(min(len(ws), remaining))]
    return out


def _peer_kernel(tb, et, ts, final,
                 x_ref, g_ref, wqt_ref, keys_ref, u_ref, vt_ref, vtp_ref, fg_ref,
                 o_ref,
                 xnt_ref, qt_ref, s_ref, sv_ref, a_ref, b_ref, e2_ref, ht_ref, ct_ref, acc_ref):
    npl = tb // PRO_LANES
    rows_per_tile = et // N_KEYS
    j = pl.program_id(1)

    @pl.when(j == 0)
    def _():
        xn = _rms(x_ref[...], g_ref[...])
        xnt_ref[...] = xn.T.astype(BF16)
        qt_ref[...] = jnp.dot(wqt_ref[...], xnt_ref[...], preferred_element_type=F32).astype(BF16)
        for hc in range(2 * PEER_HEADS):
            s = jnp.dot(keys_ref[hc], qt_ref[hc * KEY_DIM:(hc + 1) * KEY_DIM, :],
                        preferred_element_type=F32)
            for pt in range(npl):
                s_ref[hc, pt] = s[:, pt * PRO_LANES:(pt + 1) * PRO_LANES]

        def select_body(idx, carry):
            h = idx // npl
            pt = idx % npl
            for hc in (2 * h, 2 * h + 1):
                ws = _sort_desc([s_ref[hc, pt, i * SUBLANES:(i + 1) * SUBLANES, :]
                                 for i in range(N_KEYS // SUBLANES)])
                rows = _top_rows(ws, N_TOP)
                sv_ref[hc, pt] = jnp.full((SV_ROWS, PRO_LANES), NEG_INF, F32)
                for k in range(N_TOP):
                    sv_ref[hc, pt, k:k + 1, :] = rows[k]
            sv1 = sv_ref[2 * h, pt]
            sv2 = sv_ref[2 * h + 1, pt]
            pieces = [sv1[0:1, :] + sv2[i * SUBLANES:(i + 1) * SUBLANES, :] for i in range(SV_ROWS // SUBLANES)]
            pieces += [sv1[a:a + 1, :] + sv2[0:SUBLANES, :] for a in range(1, SUBLANES)]
            pieces += [sv1[i * SUBLANES:(i + 1) * SUBLANES, :] + sv2[0:1, :] for i in range(1, SV_ROWS // SUBLANES)]
            best = _top_rows(_sort_desc(pieces), N_TOP)
            zsum = jnp.ones_like(best[0])
            for k in range(1, PEER_TOPK):
                zsum = zsum + jnp.exp(best[k] - best[0])
            thr = 0.5 * (best[PEER_TOPK - 1] + best[PEER_TOPK])
            s1 = s_ref[2 * h, pt]
            s2 = s_ref[2 * h + 1, pt]
            a_ref[h, pt] = jnp.exp(s1 - sv1[0:1, :]) / zsum
            b_ref[h, pt] = thr - s1
            e2_ref[h, pt] = jnp.exp(s2 - sv2[0:1, :])
            return carry

        lax.fori_loop(0, PEER_HEADS * npl, select_body, 0)
        acc_ref[...] = jnp.zeros_like(acc_ref)
        ct_ref[(ts - 1) * et:ts * et, tb - MXU_COLS:tb] = jnp.zeros((et, MXU_COLS), BF16)

    pieces = [(t, c) for t in range(ts) for c in range(tb // MXU_COLS)]
    units = [(t, c, rr) for (t, c) in pieces for rr in range(rows_per_tile)]
    out_rows = D_MODEL // rows_per_tile

    def pre_activations(t, c, rr):
        er = slice(t * et + rr * N_KEYS, t * et + (rr + 1) * N_KEYS)
        cols = slice(c * MXU_COLS, (c + 1) * MXU_COLS)
        ht_ref[er, cols] = jnp.dot(u_ref[er, :], xnt_ref[:, cols], preferred_element_type=F32)

    def output_term(t, c, m, v_ref=None):
        rows = slice(t * et, (t + 1) * et)
        cols = slice(c * MXU_COLS, (c + 1) * MXU_COLS)
        mr = slice(m * out_rows, (m + 1) * out_rows)
        v = vt_ref[t, mr, :] if v_ref is None else v_ref[0, mr, :]
        acc_ref[mr, cols] += jnp.dot(v, ct_ref[rows, cols], preferred_element_type=F32)

    def coefficient_tile(t, c, rr, l):
        i1 = (j * ts + t) * rows_per_tile + rr
        lt = c * (MXU_COLS // LANES) + l
        pt = lt // 2
        ln = slice((lt % 2) * LANES, (lt % 2 + 1) * LANES)
        g = jnp.zeros((N_KEYS, LANES), F32)
        for h in range(PEER_HEADS):
            a_row = a_ref[h, pt, pl.ds(i1, 1), :][:, ln]
            b_row = b_ref[h, pt, pl.ds(i1, 1), :][:, ln]
            g = g + jnp.where(s_ref[2 * h + 1, pt, :, ln] >= b_row, e2_ref[h, pt, :, ln], 0.0) * a_row
        er = slice(t * et + rr * N_KEYS, t * et + (rr + 1) * N_KEYS)
        tl = slice(lt * LANES, (lt + 1) * LANES)
        ct_ref[er, tl] = (g * _gelu_sigmoid_form(ht_ref[er, tl])).astype(BF16)

    pre_activations(*units[0])
    for n, (t, c, rr) in enumerate(units):
        if n + 1 < len(units):
            pre_activations(*units[n + 1])
        coefficient_tile(t, c, rr, 0)
        if n >= rows_per_tile:
            pt_, pc_, _ = units[n - rows_per_tile]
            output_term(pt_, pc_, rr)
        else:
            output_term(*pieces[-1], rr, v_ref=vtp_ref)
        coefficient_tile(t, c, rr, 1)

    @pl.when(j == pl.num_programs(1) - 1)
    def _():
        for m in range(rows_per_tile):
            output_term(*pieces[-1], m)
        y = x_ref[...] + acc_ref[...].T
        if final:
            y = _rms(y, fg_ref[...])
        o_ref[...] = y


def _peer(x, g, wqt, keys, u_tab, vt_tiles, final_g, final, tb, et, ts):
    n = x.shape[0]
    npl = tb // PRO_LANES
    assert n % tb == 0 and tb % PRO_LANES == 0 and tb % MXU_COLS == 0
    assert et % N_KEYS == 0 and N_EXPERTS % (ts * et) == 0 and D_MODEL % (et // N_KEYS) == 0
    return pl.pallas_call(
        functools.partial(_peer_kernel, tb, et, ts, final),
        grid=(n // tb, N_EXPERTS // (ts * et)),
        in_specs=[
            pl.BlockSpec((tb, D_MODEL), lambda i, j: (i, 0)),
            pl.BlockSpec((1, D_MODEL), lambda i, j: (0, 0)),
            pl.BlockSpec(wqt.shape, lambda i, j: (0, 0)),
            pl.BlockSpec(keys.shape, lambda i, j: (0, 0, 0)),
            pl.BlockSpec((ts * et, D_MODEL), lambda i, j: (j, 0)),
            pl.BlockSpec((ts, D_MODEL, et), lambda i, j: (j, 0, 0)),
            pl.BlockSpec((1, D_MODEL, et), lambda i, j: (jnp.maximum(j * ts - 1, 0), 0, 0)),
            pl.BlockSpec((1, D_MODEL), lambda i, j: (0, 0)),
        ],
        out_specs=pl.BlockSpec((tb, D_MODEL), lambda i, j: (i, 0)),
        out_shape=jax.ShapeDtypeStruct((n, D_MODEL), F32),
        scratch_shapes=[
            pltpu.VMEM((D_MODEL, tb), BF16),
            pltpu.VMEM((2 * PEER_HEADS * KEY_DIM, tb), BF16),
            pltpu.VMEM((2 * PEER_HEADS, npl, N_KEYS, PRO_LANES), F32),
            pltpu.VMEM((2 * PEER_HEADS, npl, SV_ROWS, PRO_LANES), F32),
            pltpu.VMEM((PEER_HEADS, npl, N_KEYS, PRO_LANES), F32),
            pltpu.VMEM((PEER_HEADS, npl, N_KEYS, PRO_LANES), F32),
            pltpu.VMEM((PEER_HEADS, npl, N_KEYS, PRO_LANES), F32),
            pltpu.VMEM((ts * et, tb), F32),
            pltpu.VMEM((ts * et, tb), BF16),
            pltpu.VMEM((D_MODEL, tb), F32),
        ],
        compiler_params=pltpu.CompilerParams(
            dimension_semantics=("arbitrary", "arbitrary"), vmem_limit_bytes=VMEM_LIMIT),
        name="peer_final" if final else "peer",
    )(x, g, wqt, keys, u_tab, vt_tiles, vt_tiles, final_g)


PROMPT_NB = 8
PROMPT_TT = 32
SAMPLE_NB = 32
CAST_ROWS_W_IN = 256
CAST_ROWS_TABLE = 2048
INPROJ_TM = 1024
INPROJ_TN = 2304
PEER_TB = 512
PEER_ET = 512
PEER_TS = 4


def _to_chunks(a, nb):
    b, t, c = a.shape
    return a.reshape(b // nb, nb, t, c).transpose(0, 2, 1, 3).reshape(b * t, c)


def _from_chunks(a, nb, b, t):
    c = a.shape[-1]
    return a.reshape(b // nb, t, nb, c).transpose(0, 2, 1, 3).reshape(b, t, c)


def kernel(x_prompt, x_sample, state_conv_a, state_pool, state_conv_c, state_lru_h, state_conv_d, norm1_g, w_in, b_gate, conv_a_w, pool_w, pool_scale, conv_c_w, conv_c_b, lru_wa, lru_ba, lru_wi, lru_bi, lru_lambda, conv_d_w, conv_d_b, ln_d_g, ln_d_b, w_pa, w_pb, w_pc, w_pd, w_out, norm2_g, peer_wq, peer_keys, peer_u, peer_v, final_g):
    bp, tp, _ = x_prompt.shape
    bs, ts, _ = x_sample.shape
    xp = _to_chunks(x_prompt, PROMPT_NB)
    xs = _to_chunks(x_sample, SAMPLE_NB)
    fg = final_g.reshape(1, D_MODEL)

    outs_p = [[] for _ in range(5)]
    outs_s = [[] for _ in range(5)]
    for l in range(DEPTH):
        lp = {
            'conv_a_w': conv_a_w[l], 'pool_w': pool_w[l].astype(BF16), 'pool_scale': pool_scale[l].reshape(1, W_B),
            'conv_c_w': conv_c_w[l], 'conv_c_b': conv_c_b[l].reshape(1, W_C),
            'lru_wa': lru_wa[l].astype(BF16), 'lru_ba': lru_ba[l].reshape(1, W_C),
            'lru_wi': lru_wi[l].astype(BF16), 'lru_bi': lru_bi[l].reshape(1, W_C),
            'lru_lambda': lru_lambda[l].reshape(1, W_C),
            'conv_d_w': conv_d_w[l], 'conv_d_b': conv_d_b[l].reshape(1, W_D),
            'ln_d_g': ln_d_g[l].reshape(1, W_D), 'ln_d_b': ln_d_b[l].reshape(1, W_D),
            'b_gate': b_gate[l].reshape(1, 4 * D_MODEL),
            'w_pa': w_pa[l].astype(BF16), 'w_pb': w_pb[l].astype(BF16), 'w_pc': w_pc[l].astype(BF16),
            'w_pd': w_pd[l].astype(BF16), 'w_out': w_out[l].astype(BF16),
        }
        g1 = norm1_g[l].reshape(1, D_MODEL)
        g2 = norm2_g[l].reshape(1, D_MODEL)
        w_in_b = _cast_bf16(w_in, l, CAST_ROWS_W_IN)
        wqt = peer_wq[l].T.astype(BF16)
        keys = peer_keys[l].reshape(2 * PEER_HEADS, N_KEYS, KEY_DIM).astype(BF16)
        u_tab = _cast_bf16(peer_u, l, CAST_ROWS_TABLE)
        vt_tab = _cast_transpose_tiles(peer_v, l, PEER_ET)
        final = l == DEPTH - 1

        zero_hist = (jnp.zeros(((CONV_A - 1) * bp, W_A), F32), jnp.zeros((POOL_BUF * bp, W_B), F32),
                     jnp.zeros(((CONV_C - 1) * bp, W_C), F32), jnp.zeros((bp, W_C), F32),
                     jnp.zeros(((CONV_D - 1) * bp, W_D), F32))
        u = _inproj(xp, g1, w_in_b, INPROJ_TM, INPROJ_TN)
        xp, *st = _mixer(u, xp, zero_hist, lp, PROMPT_NB, PROMPT_TT, True, 0)
        for acc, s in zip(outs_p, st):
            acc.append(s)
        xp = _peer(xp, g2, wqt, keys, u_tab, vt_tab, fg, final, PEER_TB, PEER_ET, PEER_TS)

        hist = (_to_chunks(state_conv_a[l], SAMPLE_NB), _to_chunks(state_pool[l], SAMPLE_NB),
                _to_chunks(state_conv_c[l], SAMPLE_NB), state_lru_h[l], _to_chunks(state_conv_d[l], SAMPLE_NB))
        u = _inproj(xs, g1, w_in_b, min(INPROJ_TM, xs.shape[0]), INPROJ_TN)
        xs, *st = _mixer(u, xs, hist, lp, SAMPLE_NB, ts, False, PAST_LEN)
        for acc, s in zip(outs_s, st):
            acc.append(s)
        xs = _peer(xs, g2, wqt, keys, u_tab, vt_tab, fg, final, PEER_TB, PEER_ET, PEER_TS)

    y_prompt = _from_chunks(xp, PROMPT_NB, bp, tp)
    y_sample = _from_chunks(xs, SAMPLE_NB, bs, ts)

    def states(outs, nb, b):
        res = []
        for idx, hist_len in ((0, CONV_A - 1), (1, POOL_BUF), (2, CONV_C - 1), (3, None), (4, CONV_D - 1)):
            if hist_len is None:
                res.append(jnp.stack(outs[idx]))
            else:
                res.append(jnp.stack([_from_chunks(s, nb, b, hist_len) for s in outs[idx]]))
        return res

    ca_p, pool_p, cc_p, h_p, cd_p = states(outs_p, PROMPT_NB, bp)
    ca_s, pool_s, cc_s, h_s, cd_s = states(outs_s, SAMPLE_NB, bs)
    return (y_prompt, y_sample, ca_p, ca_s, pool_p, pool_s, cc_p, cc_s, h_p, h_s, cd_p, cd_s)
```

```python
import functools

import jax
import jax.numpy as jnp
from jax import lax
from jax.experimental import pallas as pl
from jax.experimental.pallas import tpu as pltpu

F32 = jnp.float32
BF16 = jnp.bfloat16

D_MODEL = 1024
DEPTH = 2
PAST_LEN = 16384
W_A = 512
CONV_A = 3
W_B = 512
POOL_WINDOWS = (2, 4, 8, 16)
POOL_GROUP = 128
POOL_BUF = 15
W_C = 1024
LRU_HEADS = 8
LRU_HEAD_DIM = 128
CONV_C = 4
LRU_C = 8.0
W_D = 512
CONV_D = 31
OFF_B = 3 * W_A
OFF_C = OFF_B + W_B
OFF_D = OFF_C + 2 * W_C
OFF_G = OFF_D + 2 * W_D
IN_TOTAL = OFF_G + 4 * D_MODEL
PEER_HEADS = 8
N_KEYS = 128
N_EXPERTS = N_KEYS * N_KEYS
KEY_DIM = 128
PEER_TOPK = 16
EPS = 1e-6

LANES = 128
VMEM_LIMIT = 56 * 1024 * 1024

GELU_C0 = 0.7978845608028654
GELU_C1 = 0.044715
LOG2_E = 1.4426950408889634
NEG_INF = float("-inf")


def _sigmoid(x):
    return 1.0 / (1.0 + jnp.exp2(x * (-LOG2_E)))


def _gelu_sigmoid_form(x):
    q = x * (x * x * (-2.0 * GELU_C0 * GELU_C1 * LOG2_E) + (-2.0 * GELU_C0 * LOG2_E))
    return x / (1.0 + jnp.exp2(q))


def _rms(x, g):
    ms = jnp.mean(x * x, axis=-1, keepdims=True)
    return x * lax.rsqrt(ms + EPS) * g


def _bdot(a, b):
    return jnp.dot(a.astype(BF16), b.astype(BF16), preferred_element_type=F32)


def _inproj_kernel(x_ref, g_ref, w_ref, u_ref, xn_ref):
    @pl.when(pl.program_id(1) == 0)
    def _():
        xn_ref[...] = _rms(x_ref[...], g_ref[...]).astype(BF16)

    u_ref[...] = jnp.dot(xn_ref[...], w_ref[...], preferred_element_type=F32)


def _inproj(x, g, w_bf16, tm, tn):
    n = x.shape[0]
    assert n % tm == 0 and IN_TOTAL % tn == 0 and tn % LANES == 0
    return pl.pallas_call(
        _inproj_kernel,
        grid=(n // tm, IN_TOTAL // tn),
        in_specs=[
            pl.BlockSpec((tm, D_MODEL), lambda i, j: (i, 0)),
            pl.BlockSpec((1, D_MODEL), lambda i, j: (0, 0)),
            pl.BlockSpec((D_MODEL, tn), lambda i, j: (0, j)),
        ],
        out_specs=pl.BlockSpec((tm, tn), lambda i, j: (i, j)),
        out_shape=jax.ShapeDtypeStruct((n, IN_TOTAL), F32),
        scratch_shapes=[pltpu.VMEM((tm, D_MODEL), BF16)],
        compiler_params=pltpu.CompilerParams(
            dimension_semantics=("arbitrary", "arbitrary"), vmem_limit_bytes=VMEM_LIMIT),
        name="inproj",
    )(x, g, w_bf16)


def _cast_kernel(x_ref, o_ref):
    o_ref[...] = x_ref[...].astype(BF16)


def _cast_bf16(a, layer, rows):
    _, n, c = a.shape
    return pl.pallas_call(
        _cast_kernel,
        grid=(n // rows,),
        in_specs=[pl.BlockSpec((None, rows, c), lambda i: (layer, i, 0))],
        out_specs=pl.BlockSpec((rows, c), lambda i: (i, 0)),
        out_shape=jax.ShapeDtypeStruct((n, c), BF16),
        compiler_params=pltpu.CompilerParams(dimension_semantics=("arbitrary",), vmem_limit_bytes=VMEM_LIMIT),
        name="cast_bf16",
    )(a)


def _cast_transpose_kernel(x_ref, o_ref):
    o_ref[0] = x_ref[...].T.astype(BF16)


def _cast_transpose_tiles(a, layer, rows):
    _, n, c = a.shape
    return pl.pallas_call(
        _cast_transpose_kernel,
        grid=(n // rows,),
        in_specs=[pl.BlockSpec((None, rows, c), lambda i: (layer, i, 0))],
        out_specs=pl.BlockSpec((1, c, rows), lambda i: (i, 0, 0)),
        out_shape=jax.ShapeDtypeStruct((n // rows, c, rows), BF16),
        compiler_params=pltpu.CompilerParams(dimension_semantics=("arbitrary",), vmem_limit_bytes=VMEM_LIMIT),
        name="cast_transpose_tiles",
    )(a)


def _mixer_kernel(nb, tt, carry, offset,
                  u_ref, x_ref, ha_ref, hb_ref, hc_ref, h0_ref, hd_ref,
                  caw_ref, pw_ref, ps_ref, ccw_ref, ccb_ref, wa_ref, ba_ref, wi_ref, bi_ref, lam_ref,
                  cdw_ref, cdb_ref, lng_ref, lnb_ref, bg_ref,
                  wpa_ref, wpb_ref, wpc_ref, wpd_ref, wout_ref,
                  xo_ref, sa_ref, sb_ref, sc_ref, hl_ref, sd_ref,
                  ea_ref, eb_ref, ec_ref, ed_ref, h_ref, a_ref, bx_ref, hs_ref, yb_ref, gate_ref):
    r = tt * nb
    step = pl.program_id(0)
    first = (step == 0) if carry else None

    def load_hist():
        ea_ref[0:(CONV_A - 1) * nb, :] = ha_ref[...]
        eb_ref[0:POOL_BUF * nb, :] = hb_ref[...]
        ec_ref[0:(CONV_C - 1) * nb, :] = hc_ref[...]
        ed_ref[0:(CONV_D - 1) * nb, :] = hd_ref[...]
        h_ref[...] = h0_ref[...]

    if carry:
        pl.when(first)(load_hist)
    else:
        load_hist()

    z = u_ref[:, W_A:2 * W_A] * u_ref[:, 2 * W_A:3 * W_A]
    ea_ref[(CONV_A - 1) * nb:, :] = z
    y = caw_ref[0:1, :] * ea_ref[0:r, :]
    for k in range(1, CONV_A):
        y = y + caw_ref[k:k + 1, :] * ea_ref[k * nb:k * nb + r, :]
    ya = u_ref[:, 0:W_A] * y

    eb_ref[POOL_BUF * nb:, :] = u_ref[:, OFF_B:OFF_C]
    row = lax.broadcasted_iota(jnp.int32, (r, POOL_GROUP), 0)
    tpos = (row >> (nb.bit_length() - 1)) + (step * tt if carry else 0) + offset
    for g, w in enumerate(POOL_WINDOWS):
        sl = slice(g * POOL_GROUP, (g + 1) * POOL_GROUP)
        acc = eb_ref[POOL_BUF * nb:POOL_BUF * nb + r, sl]
        for k in range(1, w):
            acc = acc + eb_ref[(POOL_BUF - k) * nb:(POOL_BUF - k) * nb + r, sl]
        cnt = jnp.minimum(tpos + 1, w).astype(F32)
        p = acc / cnt - eb_ref[POOL_BUF * nb:POOL_BUF * nb + r, sl]
        yb_ref[:, sl] = _bdot(p, pw_ref[g]) * ps_ref[:, sl]

    ec_ref[(CONV_C - 1) * nb:, :] = u_ref[:, OFF_C:OFF_C + W_C]
    xconv = ccw_ref[0:1, :] * ec_ref[0:r, :]
    for k in range(1, CONV_C):
        xconv = xconv + ccw_ref[k:k + 1, :] * ec_ref[k * nb:k * nb + r, :]
    xconv = xconv + ccb_ref[...]
    lam = lam_ref[...]
    nsp = -LRU_C * (jnp.maximum(-lam, 0.0) + jnp.log(1.0 + jnp.exp(-jnp.abs(lam))))
    for hh in range(LRU_HEADS):
        sl = slice(hh * LRU_HEAD_DIM, (hh + 1) * LRU_HEAD_DIM)
        xh = xconv[:, sl]
        xhb = xh.astype(BF16)
        rg = _sigmoid(jnp.dot(xhb, wa_ref[hh], preferred_element_type=F32) + ba_ref[:, sl])
        ig = _sigmoid(jnp.dot(xhb, wi_ref[hh], preferred_element_type=F32) + bi_ref[:, sl])
        log_a = rg * nsp[:, sl]
        a = jnp.exp(log_a)
        a_ref[:, sl] = a
        bx_ref[:, sl] = jnp.sqrt(1.0 - a * a) * (ig * xh)

    if tt <= 8:
        for t in range(tt):
            h = a_ref[t * nb:(t + 1) * nb, :] * h_ref[...] + bx_ref[t * nb:(t + 1) * nb, :]
            h_ref[...] = h
            hs_ref[t * nb:(t + 1) * nb, :] = h
    else:
        def scan_body(t, h):
            rows = pl.ds(pl.multiple_of(t * nb, nb), nb)
            h = a_ref[rows, :] * h + bx_ref[rows, :]
            hs_ref[rows, :] = h
            return h
        h_ref[...] = lax.fori_loop(0, tt, scan_body, h_ref[...])
    yc = hs_ref[...] * _gelu_sigmoid_form(u_ref[:, OFF_C + W_C:OFF_D])

    ed_ref[(CONV_D - 1) * nb:, :] = u_ref[:, OFF_D:OFF_D + W_D] * _sigmoid(u_ref[:, OFF_D + W_D:OFF_G])
    c = cdw_ref[0:1, :] * ed_ref[0:r, :]
    for k in range(1, CONV_D):
        c = c + cdw_ref[k:k + 1, :] * ed_ref[k * nb:k * nb + r, :]
    c = c + cdb_ref[...]
    mu = jnp.mean(c, axis=-1, keepdims=True)
    cc = c - mu
    var = jnp.mean(cc * cc, axis=-1, keepdims=True)
    ln = cc * lax.rsqrt(var + EPS) * lng_ref[...] + lnb_ref[...]
    yd = ln * _sigmoid(ln)

    gate_ref[...] = _sigmoid(u_ref[:, OFF_G:] + bg_ref[...])
    m = gate_ref[:, 0:D_MODEL] * _bdot(ya, wpa_ref[...])
    m = m + gate_ref[:, D_MODEL:2 * D_MODEL] * _bdot(yb_ref[...], wpb_ref[...])
    m = m + gate_ref[:, 2 * D_MODEL:3 * D_MODEL] * _bdot(yc, wpc_ref[...])
    m = m + gate_ref[:, 3 * D_MODEL:] * _bdot(yd, wpd_ref[...])
    xo_ref[...] = x_ref[...] + _bdot(m, wout_ref[...])

    def store_state():
        sa_ref[...] = ea_ref[r:r + (CONV_A - 1) * nb, :]
        sb_ref[...] = eb_ref[r:r + POOL_BUF * nb, :]
        sc_ref[...] = ec_ref[r:r + (CONV_C - 1) * nb, :]
        sd_ref[...] = ed_ref[r:r + (CONV_D - 1) * nb, :]
        hl_ref[...] = h_ref[...]

    if carry:
        pl.when(step == pl.num_programs(0) - 1)(store_state)
        for e_ref, hist in ((ea_ref, CONV_A - 1), (eb_ref, POOL_BUF), (ec_ref, CONV_C - 1), (ed_ref, CONV_D - 1)):
            total = hist * nb
            for s in range(0, total, r):
                n = min(r, total - s)
                e_ref[s:s + n, :] = e_ref[r + s:r + s + n, :]
    else:
        store_state()


def _mixer(u, x, hists, lp, nb, tt, carry, offset):
    n = x.shape[0]
    r = nb * tt
    steps = n // r
    assert n % r == 0 and nb % 8 == 0 and nb & (nb - 1) == 0
    ha, hb, hc, h0, hd = hists
    widths = (W_A, W_B, W_C, W_D)
    hist_rows = ((CONV_A - 1) * nb, POOL_BUF * nb, (CONV_C - 1) * nb, (CONV_D - 1) * nb)

    def chunk_map(i):
        return (0, 0) if carry else (i, 0)

    def const2(i):
        return (0, 0)

    def const3(i):
        return (0, 0, 0)

    def full(a):
        return pl.BlockSpec(a.shape, const2 if a.ndim == 2 else const3)

    params = [lp['conv_a_w'], lp['pool_w'], lp['pool_scale'], lp['conv_c_w'], lp['conv_c_b'],
              lp['lru_wa'], lp['lru_ba'], lp['lru_wi'], lp['lru_bi'], lp['lru_lambda'],
              lp['conv_d_w'], lp['conv_d_b'], lp['ln_d_g'], lp['ln_d_b'], lp['b_gate'],
              lp['w_pa'], lp['w_pb'], lp['w_pc'], lp['w_pd'], lp['w_out']]
    in_specs = [
        pl.BlockSpec((r, IN_TOTAL), lambda i: (i, 0)),
        pl.BlockSpec((r, D_MODEL), lambda i: (i, 0)),
        pl.BlockSpec((hist_rows[0], W_A), chunk_map),
        pl.BlockSpec((hist_rows[1], W_B), chunk_map),
        pl.BlockSpec((hist_rows[2], W_C), chunk_map),
        pl.BlockSpec((nb, W_C), chunk_map),
        pl.BlockSpec((hist_rows[3], W_D), chunk_map),
    ] + [full(p) for p in params]
    n_state = 1 if carry else steps
    out_shape = (
        jax.ShapeDtypeStruct((n, D_MODEL), F32),
        jax.ShapeDtypeStruct((n_state * hist_rows[0], W_A), F32),
        jax.ShapeDtypeStruct((n_state * hist_rows[1], W_B), F32),
        jax.ShapeDtypeStruct((n_state * hist_rows[2], W_C), F32),
        jax.ShapeDtypeStruct((n_state * nb, W_C), F32),
        jax.ShapeDtypeStruct((n_state * hist_rows[3], W_D), F32),
    )
    out_specs = (
        pl.BlockSpec((r, D_MODEL), lambda i: (i, 0)),
        pl.BlockSpec((hist_rows[0], W_A), chunk_map),
        pl.BlockSpec((hist_rows[1], W_B), chunk_map),
        pl.BlockSpec((hist_rows[2], W_C), chunk_map),
        pl.BlockSpec((nb, W_C), chunk_map),
        pl.BlockSpec((hist_rows[3], W_D), chunk_map),
    )
    scratch = [pltpu.VMEM((hr + r, w), F32) for hr, w in zip(hist_rows, widths)]
    scratch += [
        pltpu.VMEM((nb, W_C), F32),
        pltpu.VMEM((r, W_C), F32),
        pltpu.VMEM((r, W_C), F32),
        pltpu.VMEM((r, W_C), F32),
        pltpu.VMEM((r, W_B), F32),
        pltpu.VMEM((r, 4 * D_MODEL), F32),
    ]
    return pl.pallas_call(
        functools.partial(_mixer_kernel, nb, tt, carry, offset),
        grid=(steps,),
        in_specs=in_specs,
        out_specs=out_specs,
        out_shape=out_shape,
        scratch_shapes=scratch,
        compiler_params=pltpu.CompilerParams(
            dimension_semantics=("arbitrary",), vmem_limit_bytes=VMEM_LIMIT),
        name="mixer_carry" if carry else "mixer_batch",
    )(u, x, ha, hb, hc, h0, hd, *params)


N_TOP = PEER_TOPK + 1
SV_ROWS = 24


PRO_LANES = 2 * LANES
SUBLANES = 8
MXU_COLS = 256


def _batcher_pairs(n):
    pairs = []

    def merge(lo, hi, r):
        step = r * 2
        if step < hi - lo:
            merge(lo, hi, step)
            merge(lo + r, hi, step)
            pairs.extend((i, i + r) for i in range(lo + r, hi - r, step))
        else:
            pairs.append((lo, lo + r))

    def sort(lo, hi):
        if hi - lo >= 1:
            mid = lo + (hi - lo) // 2
            sort(lo, mid)
            sort(mid + 1, hi)
            merge(lo, hi, 1)

    sort(0, n - 1)
    return pairs


def _sort_desc(ws):
    n = 1
    while n < len(ws):
        n *= 2
    ws = list(ws) + [None] * (n - len(ws))
    for i, j in _batcher_pairs(n):
        a, b = ws[i], ws[j]
        if b is None:
            continue
        if a is None:
            ws[i], ws[j] = b, None
        else:
            ws[i], ws[j] = jnp.maximum(a, b), jnp.minimum(a, b)
    return [w for w in ws if w is not None]


def _top_rows(ws, n):
    out = []
    for k in range(n):
        m = jnp.max(ws[0], axis=0, keepdims=True)
        out.append(m)
        remaining = n - 1 - k
        if remaining == 0:
            break
        hit = ws[0] >= m
        ws = [jnp.where(hit, ws[i + 1] if i + 1 < len(ws) else NEG_INF, ws[i])
              for i in range(min(len(ws), remaining))]
    return out


def _peer_kernel(tb, et, ts, final,
                 x_ref, g_ref, wqt_ref, keys_ref, u_ref, vt_ref, vtp_ref, fg_ref,
                 o_ref,
                 xnt_ref, qt_ref, s_ref, sv_ref, a_ref, b_ref, e2_ref, acc_ref, *piece_refs):
    npl = tb // PRO_LANES
    rows_per_tile = et // N_KEYS
    pieces = [(t, c) for t in range(ts) for c in range(tb // MXU_COLS)]
    units = [(t, c, rr) for (t, c) in pieces for rr in range(rows_per_tile)]
    ht_refs = dict(zip(units, piece_refs[:len(units)]))
    ct_refs = dict(zip(pieces, piece_refs[len(units):]))
    j = pl.program_id(1)

    @pl.when(j == 0)
    def _():
        xn = _rms(x_ref[...], g_ref[...])
        xnt_ref[...] = xn.T.astype(BF16)
        qt_ref[...] = jnp.dot(wqt_ref[...], xnt_ref[...], preferred_element_type=F32).astype(BF16)
        for hc in range(2 * PEER_HEADS):
            s = jnp.dot(keys_ref[hc], qt_ref[hc * KEY_DIM:(hc + 1) * KEY_DIM, :],
                        preferred_element_type=F32)
            for pt in range(npl):
                s_ref[hc, pt] = s[:, pt * PRO_LANES:(pt + 1) * PRO_LANES]

        def select_body(idx, carry):
            h = idx // npl
            pt = idx % npl
            for hc in (2 * h, 2 * h + 1):
                ws = _sort_desc([s_ref[hc, pt, i * SUBLANES:(i + 1) * SUBLANES, :]
                                 for i in range(N_KEYS // SUBLANES)])
                rows = _top_rows(ws, N_TOP)
                sv_ref[hc, pt] = jnp.full((SV_ROWS, PRO_LANES), NEG_INF, F32)
                for k in range(N_TOP):
                    sv_ref[hc, pt, k:k + 1, :] = rows[k]
            sv1 = sv_ref[2 * h, pt]
            sv2 = sv_ref[2 * h + 1, pt]
            pieces = [sv1[0:1, :] + sv2[i * SUBLANES:(i + 1) * SUBLANES, :] for i in range(SV_ROWS // SUBLANES)]
            pieces += [sv1[a:a + 1, :] + sv2[0:SUBLANES, :] for a in range(1, SUBLANES)]
            pieces += [sv1[i * SUBLANES:(i + 1) * SUBLANES, :] + sv2[0:1, :] for i in range(1, SV_ROWS // SUBLANES)]
            best = _top_rows(_sort_desc(pieces), N_TOP)
            zsum = jnp.ones_like(best[0])
            for k in range(1, PEER_TOPK):
                zsum = zsum + jnp.exp(best[k] - best[0])
            thr = 0.5 * (best[PEER_TOPK - 1] + best[PEER_TOPK])
            s1 = s_ref[2 * h, pt]
            s2 = s_ref[2 * h + 1, pt]
            a_ref[h, pt] = jnp.exp(s1 - sv1[0:1, :]) / zsum
            b_ref[h, pt] = thr - s1
            e2_ref[h, pt] = jnp.exp(s2 - sv2[0:1, :])
            return carry

        lax.fori_loop(0, PEER_HEADS * npl, select_body, 0)
        acc_ref[...] = jnp.zeros_like(acc_ref)
        ct_refs[pieces[-1]][...] = jnp.zeros((et, MXU_COLS), BF16)

    out_rows = D_MODEL // rows_per_tile

    def pre_activations(t, c, rr):
        er = slice(t * et + rr * N_KEYS, t * et + (rr + 1) * N_KEYS)
        cols = slice(c * MXU_COLS, (c + 1) * MXU_COLS)
        ht_refs[(t, c, rr)][...] = jnp.dot(u_ref[er, :], xnt_ref[:, cols], preferred_element_type=F32)

    def output_term(t, c, m, v_ref=None):
        cols = slice(c * MXU_COLS, (c + 1) * MXU_COLS)
        mr = slice(m * out_rows, (m + 1) * out_rows)
        v = vt_ref[t, mr, :] if v_ref is None else v_ref[0, mr, :]
        acc_ref[mr, cols] += jnp.dot(v, ct_refs[(t, c)][...], preferred_element_type=F32)

    def coefficient_tile(t, c, rr, l):
        i1 = (j * ts + t) * rows_per_tile + rr
        lt = c * (MXU_COLS // LANES) + l
        pt = lt // 2
        ln = slice((lt % 2) * LANES, (lt % 2 + 1) * LANES)
        g = jnp.zeros((N_KEYS, LANES), F32)
        for h in range(PEER_HEADS):
            a_row = a_ref[h, pt, pl.ds(i1, 1), :][:, ln]
            b_row = b_ref[h, pt, pl.ds(i1, 1), :][:, ln]
            g = g + jnp.where(s_ref[2 * h + 1, pt, :, ln] >= b_row, e2_ref[h, pt, :, ln], 0.0) * a_row
        er = slice(rr * N_KEYS, (rr + 1) * N_KEYS)
        tl = slice(l * LANES, (l + 1) * LANES)
        ct_refs[(t, c)][er, tl] = (g * _gelu_sigmoid_form(ht_refs[(t, c, rr)][:, tl])).astype(BF16)

    pre_activations(*units[0])
    for n, (t, c, rr) in enumerate(units):
        if n + 1 < len(units):
            pre_activations(*units[n + 1])
        coefficient_tile(t, c, rr, 0)
        if n >= rows_per_tile:
            pt_, pc_, _ = units[n - rows_per_tile]
            output_term(pt_, pc_, rr)
        else:
            output_term(*pieces[-1], rr, v_ref=vtp_ref)
        coefficient_tile(t, c, rr, 1)

    @pl.when(j == pl.num_programs(1) - 1)
    def _():
        for m in range(rows_per_tile):
            output_term(*pieces[-1], m)
        y = x_ref[...] + acc_ref[...].T
        if final:
            y = _rms(y, fg_ref[...])
        o_ref[...] = y


def _peer(x, g, wqt, keys, u_tab, vt_tiles, final_g, final, tb, et, ts):
    n = x.shape[0]
    npl = tb // PRO_LANES
    assert n % tb == 0 and tb % PRO_LANES == 0 and tb % MXU_COLS == 0
    assert et % N_KEYS == 0 and N_EXPERTS % (ts * et) == 0 and D_MODEL % (et // N_KEYS) == 0
    n_pieces = ts * (tb // MXU_COLS)
    n_units = n_pieces * (et // N_KEYS)
    return pl.pallas_call(
        functools.partial(_peer_kernel, tb, et, ts, final),
        grid=(n // tb, N_EXPERTS // (ts * et)),
        in_specs=[
            pl.BlockSpec((tb, D_MODEL), lambda i, j: (i, 0)),
            pl.BlockSpec((1, D_MODEL), lambda i, j: (0, 0)),
            pl.BlockSpec(wqt.shape, lambda i, j: (0, 0)),
            pl.BlockSpec(keys.shape, lambda i, j: (0, 0, 0)),
            pl.BlockSpec((ts * et, D_MODEL), lambda i, j: (j, 0)),
            pl.BlockSpec((ts, D_MODEL, et), lambda i, j: (j, 0, 0)),
            pl.BlockSpec((1, D_MODEL, et), lambda i, j: (jnp.maximum(j * ts - 1, 0), 0, 0)),
            pl.BlockSpec((1, D_MODEL), lambda i, j: (0, 0)),
        ],
        out_specs=pl.BlockSpec((tb, D_MODEL), lambda i, j: (i, 0)),
        out_shape=jax.ShapeDtypeStruct((n, D_MODEL), F32),
        scratch_shapes=[
            pltpu.VMEM((D_MODEL, tb), BF16),
            pltpu.VMEM((2 * PEER_HEADS * KEY_DIM, tb), BF16),
            pltpu.VMEM((2 * PEER_HEADS, npl, N_KEYS, PRO_LANES), F32),
            pltpu.VMEM((2 * PEER_HEADS, npl, SV_ROWS, PRO_LANES), F32),
            pltpu.VMEM((PEER_HEADS, npl, N_KEYS, PRO_LANES), F32),
            pltpu.VMEM((PEER_HEADS, npl, N_KEYS, PRO_LANES), F32),
            pltpu.VMEM((PEER_HEADS, npl, N_KEYS, PRO_LANES), F32),
            pltpu.VMEM((D_MODEL, tb), F32),
        ] + [pltpu.VMEM((N_KEYS, MXU_COLS), F32) for _ in range(n_units)]
          + [pltpu.VMEM((et, MXU_COLS), BF16) for _ in range(n_pieces)],
        compiler_params=pltpu.CompilerParams(
            dimension_semantics=("arbitrary", "arbitrary"), vmem_limit_bytes=VMEM_LIMIT),
        name="peer_final" if final else "peer",
    )(x, g, wqt, keys, u_tab, vt_tiles, vt_tiles, final_g)


PROMPT_NB = 8
PROMPT_TT = 32
SAMPLE_NB = 32
CAST_ROWS_W_IN = 256
CAST_ROWS_TABLE = 2048
INPROJ_TM = 1024
INPROJ_TN = 2304
PEER_TB = 512
PEER_ET = 512
PEER_TS = 4


def _to_chunks(a, nb):
    b, t, c = a.shape
    return a.reshape(b // nb, nb, t, c).transpose(0, 2, 1, 3).reshape(b * t, c)


def _from_chunks(a, nb, b, t):
    c = a.shape[-1]
    return a.reshape(b // nb, t, nb, c).transpose(0, 2, 1, 3).reshape(b, t, c)


def kernel(x_prompt, x_sample, state_conv_a, state_pool, state_conv_c, state_lru_h, state_conv_d, norm1_g, w_in, b_gate, conv_a_w, pool_w, pool_scale, conv_c_w, conv_c_b, lru_wa, lru_ba, lru_wi, lru_bi, lru_lambda, conv_d_w, conv_d_b, ln_d_g, ln_d_b, w_pa, w_pb, w_pc, w_pd, w_out, norm2_g, peer_wq, peer_keys, peer_u, peer_v, final_g):
    bp, tp, _ = x_prompt.shape
    bs, ts, _ = x_sample.shape
    xp = _to_chunks(x_prompt, PROMPT_NB)
    xs = _to_chunks(x_sample, SAMPLE_NB)
    fg = final_g.reshape(1, D_MODEL)

    outs_p = [[] for _ in range(5)]
    outs_s = [[] for _ in range(5)]
    for l in range(DEPTH):
        lp = {
            'conv_a_w': conv_a_w[l], 'pool_w': pool_w[l].astype(BF16), 'pool_scale': pool_scale[l].reshape(1, W_B),
            'conv_c_w': conv_c_w[l], 'conv_c_b': conv_c_b[l].reshape(1, W_C),
            'lru_wa': lru_wa[l].astype(BF16), 'lru_ba': lru_ba[l].reshape(1, W_C),
            'lru_wi': lru_wi[l].astype(BF16), 'lru_bi': lru_bi[l].reshape(1, W_C),
            'lru_lambda': lru_lambda[l].reshape(1, W_C),
            'conv_d_w': conv_d_w[l], 'conv_d_b': conv_d_b[l].reshape(1, W_D),
            'ln_d_g': ln_d_g[l].reshape(1, W_D), 'ln_d_b': ln_d_b[l].reshape(1, W_D),
            'b_gate': b_gate[l].reshape(1, 4 * D_MODEL),
            'w_pa': w_pa[l].astype(BF16), 'w_pb': w_pb[l].astype(BF16), 'w_pc': w_pc[l].astype(BF16),
            'w_pd': w_pd[l].astype(BF16), 'w_out': w_out[l].astype(BF16),
        }
        g1 = norm1_g[l].reshape(1, D_MODEL)
        g2 = norm2_g[l].reshape(1, D_MODEL)
        w_in_b = _cast_bf16(w_in, l, CAST_ROWS_W_IN)
        wqt = peer_wq[l].T.astype(BF16)
        keys = peer_keys[l].reshape(2 * PEER_HEADS, N_KEYS, KEY_DIM).astype(BF16)
        u_tab = _cast_bf16(peer_u, l, CAST_ROWS_TABLE)
        vt_tab = _cast_transpose_tiles(peer_v, l, PEER_ET)
        final = l == DEPTH - 1

        zero_hist = (jnp.zeros(((CONV_A - 1) * bp, W_A), F32), jnp.zeros((POOL_BUF * bp, W_B), F32),
                     jnp.zeros(((CONV_C - 1) * bp, W_C), F32), jnp.zeros((bp, W_C), F32),
                     jnp.zeros(((CONV_D - 1) * bp, W_D), F32))
        u = _inproj(xp, g1, w_in_b, INPROJ_TM, INPROJ_TN)
        xp, *st = _mixer(u, xp, zero_hist, lp, PROMPT_NB, PROMPT_TT, True, 0)
        for acc, s in zip(outs_p, st):
            acc.append(s)
        xp = _peer(xp, g2, wqt, keys, u_tab, vt_tab, fg, final, PEER_TB, PEER_ET, PEER_TS)

        hist = (_to_chunks(state_conv_a[l], SAMPLE_NB), _to_chunks(state_pool[l], SAMPLE_NB),
                _to_chunks(state_conv_c[l], SAMPLE_NB), state_lru_h[l], _to_chunks(state_conv_d[l], SAMPLE_NB))
        u = _inproj(xs, g1, w_in_b, min(INPROJ_TM, xs.shape[0]), INPROJ_TN)
        xs, *st = _mixer(u, xs, hist, lp, SAMPLE_NB, ts, False, PAST_LEN)
        for acc, s in zip(outs_s, st):
            acc.append(s)
        xs = _peer(xs, g2, wqt, keys, u_tab, vt_tab, fg, final, PEER_TB, PEER_ET, PEER_TS)

    y_prompt = _from_chunks(xp, PROMPT_NB, bp, tp)
    y_sample = _from_chunks(xs, SAMPLE_NB, bs, ts)

    def states(outs, nb, b):
        res = []
        for idx, hist_len in ((0, CONV_A - 1), (1, POOL_BUF), (2, CONV_C - 1), (3, None), (4, CONV_D - 1)):
            if hist_len is None:
                res.append(jnp.stack(outs[idx]))
            else:
                res.append(jnp.stack([_from_chunks(s, nb, b, hist_len) for s in outs[idx]]))
        return res

    ca_p, pool_p, cc_p, h_p, cd_p = states(outs_p, PROMPT_NB, bp)
    ca_s, pool_s, cc_s, h_s, cd_s = states(outs_s, SAMPLE_NB, bs)
    return (y_prompt, y_sample, ca_p, ca_s, pool_p, pool_s, cc_p, cc_s, h_p, h_s, cd_p, cd_s)
```
